```python
import math
import jax, jax.numpy as jnp
from jax import lax
import numpy as np

D_MODEL = 4096
BATCH = 2
SEQ = 8192
DEPTH = 1

DA_HEADS = 8
DA_HEAD_DIM = 128
DA_V_DIM = 2 * DA_HEAD_DIM
MLA_HEADS = 16
MLA_Q_RANK = 1024
MLA_KV_RANK = 512
MLA_NOPE_DIM = 128
MLA_ROPE_DIM = 64
MLA_V_DIM = 128
ROPE_THETA = 10000.0
D_FF = 11008
CONV_WIDTH = 3
Q_BLOCK = 128
LN_EPS = 1e-5
RMS_EPS = 1e-6
DEEPNORM_ALPHA = (2.0 * DEPTH) ** 0.25
DEEPNORM_BETA = (8.0 * DEPTH) ** -0.25

DA_Q_COLS = DA_HEADS * 2 * DA_HEAD_DIM
DA_K_COLS = DA_HEADS * 2 * DA_HEAD_DIM
DA_V_COLS = DA_HEADS * DA_V_DIM
GATE_COLS = 2 * D_MODEL
IN_SPLITS = [DA_Q_COLS, DA_K_COLS, DA_V_COLS, MLA_Q_RANK, MLA_KV_RANK, MLA_ROPE_DIM, GATE_COLS]
IN_COLS = sum(IN_SPLITS)
IN_OFFSETS = list(np.cumsum(IN_SPLITS)[:-1])
DA_WIDTH = DA_HEADS * DA_V_DIM
MLA_WIDTH = MLA_HEADS * MLA_V_DIM

kernel_name = "hybrid_diffattn_mla_convffn_deepnorm"


def _layernorm(x, g, b):
    xf = x.astype(jnp.float32)
    mu = jnp.mean(xf, axis=-1, keepdims=True)
    var = jnp.mean(jnp.square(xf - mu), axis=-1, keepdims=True)
    return ((xf - mu) * lax.rsqrt(var + LN_EPS) * g + b).astype(x.dtype)


def _rmsnorm(x, g):
    xf = x.astype(jnp.float32)
    return (xf * lax.rsqrt(jnp.mean(jnp.square(xf), axis=-1, keepdims=True) + RMS_EPS) * g).astype(x.dtype)


def _rope(x, positions):
    half = x.shape[-1] // 2
    inv = ROPE_THETA ** (-jnp.arange(half, dtype=jnp.float32) / half)
    ang = positions.astype(jnp.float32)[..., None] * inv
    ang = ang.reshape(ang.shape[:2] + (1,) * (x.ndim - 3) + (half,))
    cos, sin = jnp.cos(ang), jnp.sin(ang)
    xf = x.astype(jnp.float32)
    x1, x2 = xf[..., :half], xf[..., half:]
    return jnp.concatenate([x1 * cos - x2 * sin, x1 * sin + x2 * cos], axis=-1).astype(x.dtype)


def _to_blocks(a):
    b, s = a.shape[:2]
    return jnp.moveaxis(a.reshape((b, s // Q_BLOCK, Q_BLOCK) + a.shape[2:]), 1, 0)


def _from_blocks(a):
    nb, b, qb = a.shape[:3]
    return jnp.moveaxis(a, 0, 1).reshape((b, nb * qb) + a.shape[3:])


def _diff_attention(q, k, v, positions, lam):
    slopes = 2.0 ** (-8.0 * jnp.arange(1, DA_HEADS + 1, dtype=jnp.float32) / DA_HEADS)
    scale = DA_HEAD_DIM ** -0.5

    def block(args):
        qb, pq = args
        s = jnp.einsum('bqhmd,bkhmd->bhmqk', qb, k).astype(jnp.float32) * scale
        dist = pq[:, :, None] - positions[:, None, :]
        distf = dist.astype(jnp.float32)[:, None, None]
        s = jnp.where(dist[:, None, None] >= 0, s - slopes[None, :, None, None, None] * distf, -jnp.inf)
        p = jax.nn.softmax(s, axis=-1)
        a = p[:, :, 0] - lam * p[:, :, 1]
        return jnp.einsum('bhqk,bkhe->bqhe', a.astype(v.dtype), v)

    return _from_blocks(lax.map(block, (_to_blocks(q), _to_blocks(positions))))


def _mla_attention(q_nope, q_rope, k_nope, k_rope, v, positions):
    scale = (MLA_NOPE_DIM + MLA_ROPE_DIM) ** -0.5

    def block(args):
        qnb, qrb, pq = args
        s = (jnp.einsum('bqhd,bkhd->bhqk', qnb, k_nope)
             + jnp.einsum('bqhr,bkr->bhqk', qrb, k_rope)).astype(jnp.float32) * scale
        mask = (pq[:, :, None] >= positions[:, None, :])[:, None]
        p = jax.nn.softmax(jnp.where(mask, s, -jnp.inf), axis=-1)
        return jnp.einsum('bhqk,bkhe->bqhe', p.astype(v.dtype), v)

    return _from_blocks(lax.map(block, (_to_blocks(q_nope), _to_blocks(q_rope), _to_blocks(positions))))


def _conv_ffn(h, w_up, conv_w, conv_b, w_down):
    u = h @ w_up
    up = jnp.pad(u, ((0, 0), (CONV_WIDTH - 1, 0), (0, 0)))
    s = u.shape[1]
    c = conv_b + sum(conv_w[j] * up[:, j:j + s] for j in range(CONV_WIDTH))
    gate, val = c[..., :D_FF], c[..., D_FF:]
    return (jax.nn.silu(gate) * val) @ w_down


def setup_inputs(seed: int = 0) -> dict:
    key = jax.random.key(seed)
    ks = jax.random.split(key, 24)
    L, D, F = DEPTH, D_MODEL, D_FF
    nrm = lambda k, shape, fan_in: jax.random.normal(k, shape, jnp.float32) * fan_in ** -0.5
    x = jax.random.normal(ks[0], (BATCH, SEQ, D), jnp.float32)
    offset = jax.random.randint(ks[1], (BATCH, 1), 0, 1024, dtype=jnp.int32)
    positions = jnp.arange(SEQ, dtype=jnp.int32)[None, :] + offset
    col_scale = jnp.concatenate([
        jnp.ones((DA_Q_COLS + DA_K_COLS,), jnp.float32),
        jnp.full((DA_V_COLS,), DEEPNORM_BETA, jnp.float32),
        jnp.ones((MLA_Q_RANK + MLA_KV_RANK + MLA_ROPE_DIM + GATE_COLS,), jnp.float32)])
    w_in = nrm(ks[2], (L, D, IN_COLS), D) * col_scale
    b_gate = 0.02 * jax.random.normal(ks[3], (L, GATE_COLS), jnp.float32)
    da_lambda_q1 = 0.1 * jax.random.normal(ks[4], (L, DA_HEAD_DIM), jnp.float32)
    da_lambda_k1 = 0.1 * jax.random.normal(ks[5], (L, DA_HEAD_DIM), jnp.float32)
    da_lambda_q2 = 0.1 * jax.random.normal(ks[6], (L, DA_HEAD_DIM), jnp.float32)
    da_lambda_k2 = 0.1 * jax.random.normal(ks[7], (L, DA_HEAD_DIM), jnp.float32)
    da_subln_g = 1.0 + 0.02 * jax.random.normal(ks[8], (L, DA_V_DIM), jnp.float32)
    mla_q_norm_g = 1.0 + 0.02 * jax.random.normal(ks[9], (L, MLA_Q_RANK), jnp.float32)
    mla_kv_norm_g = 1.0 + 0.02 * jax.random.normal(ks[10], (L, MLA_KV_RANK), jnp.float32)
    w_uq = nrm(ks[11], (L, MLA_Q_RANK, MLA_HEADS * (MLA_NOPE_DIM + MLA_ROPE_DIM)), MLA_Q_RANK)
    ukv_scale = jnp.tile(jnp.concatenate([jnp.ones((MLA_NOPE_DIM,), jnp.float32),
                                          jnp.full((MLA_V_DIM,), DEEPNORM_BETA, jnp.float32)]), MLA_HEADS)
    w_ukv = nrm(ks[12], (L, MLA_KV_RANK, MLA_HEADS * (MLA_NOPE_DIM + MLA_V_DIM)), MLA_KV_RANK) * ukv_scale
    w_proj_a = nrm(ks[13], (L, DA_WIDTH, D), DA_WIDTH) * DEEPNORM_BETA
    w_proj_b = nrm(ks[14], (L, MLA_WIDTH, D), MLA_WIDTH) * DEEPNORM_BETA
    w_out = nrm(ks[15], (L, D, D), D) * DEEPNORM_BETA
    ln1_g = 1.0 + 0.02 * jax.random.normal(ks[16], (L, D), jnp.float32)
    ln1_b = 0.02 * jax.random.normal(ks[17], (L, D), jnp.float32)
    w_up = nrm(ks[18], (L, D, 2 * F), D) * DEEPNORM_BETA
    conv_w = nrm(ks[19], (L, CONV_WIDTH, 2 * F), CONV_WIDTH)
    conv_b = 0.02 * jax.random.normal(ks[20], (L, 2 * F), jnp.float32)
    w_down = nrm(ks[21], (L, F, D), F) * DEEPNORM_BETA
    ln2_g = 1.0 + 0.02 * jax.random.normal(ks[22], (L, D), jnp.float32)
    ln2_b = 0.02 * jax.random.normal(ks[23], (L, D), jnp.float32)
    return {"x": x, "positions": positions, "w_in": w_in, "b_gate": b_gate,
            "da_lambda_q1": da_lambda_q1, "da_lambda_k1": da_lambda_k1,
            "da_lambda_q2": da_lambda_q2, "da_lambda_k2": da_lambda_k2,
            "da_subln_g": da_subln_g, "mla_q_norm_g": mla_q_norm_g, "mla_kv_norm_g": mla_kv_norm_g,
            "w_uq": w_uq, "w_ukv": w_ukv, "w_proj_a": w_proj_a, "w_proj_b": w_proj_b,
            "w_out": w_out, "ln1_g": ln1_g, "ln1_b": ln1_b, "w_up": w_up, "conv_w": conv_w,
            "conv_b": conv_b, "w_down": w_down, "ln2_g": ln2_g, "ln2_b": ln2_b}


def reference(x, positions, w_in, b_gate, da_lambda_q1, da_lambda_k1, da_lambda_q2, da_lambda_k2,
              da_subln_g, mla_q_norm_g, mla_kv_norm_g, w_uq, w_ukv, w_proj_a, w_proj_b, w_out,
              ln1_g, ln1_b, w_up, conv_w, conv_b, w_down, ln2_g, ln2_b):
    b, s, _ = x.shape
    h = x
    for l in range(DEPTH):
        proj = h @ w_in[l]
        da_q, da_k, da_v, c_q, c_kv, k_r, gate_logits = jnp.split(proj, IN_OFFSETS, axis=-1)

        lambda_init = 0.8 - 0.6 * math.exp(-0.3 * l)
        lam = (jnp.exp(jnp.sum(da_lambda_q1[l].astype(jnp.float32) * da_lambda_k1[l].astype(jnp.float32)))
               - jnp.exp(jnp.sum(da_lambda_q2[l].astype(jnp.float32) * da_lambda_k2[l].astype(jnp.float32)))
               + lambda_init)
        qa = da_q.reshape(b, s, DA_HEADS, 2, DA_HEAD_DIM)
        ka = da_k.reshape(b, s, DA_HEADS, 2, DA_HEAD_DIM)
        va = da_v.reshape(b, s, DA_HEADS, DA_V_DIM)
        oa = _diff_attention(qa, ka, va, positions, lam)
        oa = (_rmsnorm(oa, da_subln_g[l]) * (1.0 - lambda_init)).reshape(b, s, DA_WIDTH)

        qb = (_rmsnorm(c_q, mla_q_norm_g[l]) @ w_uq[l]).reshape(b, s, MLA_HEADS, MLA_NOPE_DIM + MLA_ROPE_DIM)
        q_nope, q_rope = qb[..., :MLA_NOPE_DIM], _rope(qb[..., MLA_NOPE_DIM:], positions)
        kv = (_rmsnorm(c_kv, mla_kv_norm_g[l]) @ w_ukv[l]).reshape(b, s, MLA_HEADS, MLA_NOPE_DIM + MLA_V_DIM)
        k_nope, vb = kv[..., :MLA_NOPE_DIM], kv[..., MLA_NOPE_DIM:]
        k_rope = _rope(k_r, positions)
        ob = _mla_attention(q_nope, q_rope, k_nope, k_rope, vb, positions).reshape(b, s, MLA_WIDTH)

        g = jax.nn.sigmoid(gate_logits + b_gate[l])
        merged = g[..., :D_MODEL] * (oa @ w_proj_a[l]) + g[..., D_MODEL:] * (ob @ w_proj_b[l])
        h = _layernorm(DEEPNORM_ALPHA * h + merged @ w_out[l], ln1_g[l], ln1_b[l])

        f = _conv_ffn(h, w_up[l], conv_w[l], conv_b[l], w_down[l])
        h = _layernorm(DEEPNORM_ALPHA * h + f, ln2_g[l], ln2_b[l])
    return h
```

```python
import functools
import math

import numpy as np
import jax
import jax.numpy as jnp
from jax import lax
from jax.experimental import pallas as pl
from jax.experimental.pallas import tpu as pltpu

DA_HEADS = 8
DA_HEAD_DIM = 128
DA_V_DIM = 2 * DA_HEAD_DIM
MLA_HEADS = 16
MLA_Q_RANK = 1024
MLA_KV_RANK = 512
MLA_NOPE_DIM = 128
MLA_ROPE_DIM = 64
MLA_V_DIM = 128
ROPE_THETA = 10000.0
CONV_WIDTH = 3
LN_EPS = 1e-5
RMS_EPS = 1e-6

LANES = 128
SUBLANES = 8
VMEM_LIMIT = 56 * 1024 * 1024
NEG_BIG = -1e30
FFN_DOWN_COL_CHUNK = 1024
FFN_DOWN_ROW_CHUNK = 64

F32 = jnp.float32
BF16 = jnp.bfloat16


def _params(sem):
    return pltpu.CompilerParams(dimension_semantics=sem, vmem_limit_bytes=VMEM_LIMIT)


def _tile(n, pref):
    t = min(n, pref)
    while n % t:
        t //= 2
    return t


def _dot(a, b):
    return jnp.dot(a, b, preferred_element_type=F32)


def _dot_nt(a, b):
    return lax.dot_general(a, b, (((1,), (1,)), ((), ())), preferred_element_type=F32)


def _mm_scale_kernel(a_ref, w_ref, s_ref, o_ref):
    o_ref[...] = (_dot(a_ref[...], w_ref[...]) * s_ref[...]).astype(o_ref.dtype)


def _mm_scale(a, w, scale, out_dtype, tm, tn, name):
    m, k = a.shape
    n = w.shape[1]
    return pl.pallas_call(
        _mm_scale_kernel,
        grid=(m // tm, n // tn),
        in_specs=[pl.BlockSpec((tm, k), lambda i, j: (i, 0)),
                  pl.BlockSpec((k, tn), lambda i, j: (0, j)),
                  pl.BlockSpec((1, tn), lambda i, j: (0, j))],
        out_specs=pl.BlockSpec((tm, tn), lambda i, j: (i, j)),
        out_shape=jax.ShapeDtypeStruct((m, n), out_dtype),
        compiler_params=_params(("parallel", "arbitrary")),
        name=name,
    )(a, w, scale)


def _rms_bf16(c, g):
    ms = jnp.mean(c * c, axis=-1, keepdims=True)
    return (c * lax.rsqrt(ms + RMS_EPS) * g).astype(BF16)


def _qup_kernel(cq_ref, g_ref, pos_ref, inv_ref, wm_ref, ws_ref, o_ref, cn_ref, cos_ref, sin_ref, *, scale):
    @pl.when(pl.program_id(1) == 0)
    def _():
        cn_ref[...] = _rms_bf16(cq_ref[...], g_ref[...])
        ang = pos_ref[...] * inv_ref[...]
        cos_ref[...] = jnp.cos(ang)
        sin_ref[...] = jnp.sin(ang)

    cn = cn_ref[...]
    main = _dot(cn, wm_ref[...])
    swapped = _dot(cn, ws_ref[...])
    rope = main[:, LANES:] * cos_ref[...] + swapped * sin_ref[...]
    o_ref[:, :LANES] = (main[:, :LANES] * scale).astype(o_ref.dtype)
    o_ref[:, LANES:] = (rope * scale).astype(o_ref.dtype)


def _q_up(lat, g, posb, inv_row, w_main, w_swap, scale, tm):
    m = lat.shape[0]
    nh = MLA_HEADS
    return pl.pallas_call(
        functools.partial(_qup_kernel, scale=scale),
        grid=(m // tm, nh),
        in_specs=[pl.BlockSpec((tm, MLA_Q_RANK), lambda i, h: (i, 0)),
                  pl.BlockSpec((1, MLA_Q_RANK), lambda i, h: (0, 0)),
                  pl.BlockSpec((tm, LANES), lambda i, h: (i, 0)),
                  pl.BlockSpec((1, LANES), lambda i, h: (0, 0)),
                  pl.BlockSpec((MLA_Q_RANK, 2 * LANES), lambda i, h: (0, h)),
                  pl.BlockSpec((MLA_Q_RANK, LANES), lambda i, h: (0, h))],
        out_specs=pl.BlockSpec((tm, 2 * LANES), lambda i, h: (i, h)),
        out_shape=jax.ShapeDtypeStruct((m, nh * 2 * LANES), BF16),
        scratch_shapes=[pltpu.VMEM((tm, MLA_Q_RANK), BF16),
                        pltpu.VMEM((tm, LANES), F32),
                        pltpu.VMEM((tm, LANES), F32)],
        compiler_params=_params(("parallel", "arbitrary")),
        name="mla_q_up",
    )(lat, g, posb, inv_row, w_main, w_swap)


def _kvup_kernel(ckv_ref, kr_ref, krs_ref, g_ref, pos_ref, inv_ref, w_ref, kv_ref, krope_ref):
    cn = _rms_bf16(ckv_ref[...], g_ref[...])
    kv_ref[...] = _dot(cn, w_ref[...]).astype(kv_ref.dtype)
    ang = pos_ref[...] * inv_ref[...]
    krope_ref[...] = (kr_ref[...] * jnp.cos(ang) + krs_ref[...] * jnp.sin(ang)).astype(krope_ref.dtype)


def _kv_up(lat, g, posb, inv_row, w_ukv, tm):
    m = lat.shape[0]
    n = w_ukv.shape[1]
    ckv_blk = MLA_Q_RANK // MLA_KV_RANK
    kr_blk = (MLA_Q_RANK + MLA_KV_RANK) // LANES
    return pl.pallas_call(
        _kvup_kernel,
        grid=(m // tm,),
        in_specs=[pl.BlockSpec((tm, MLA_KV_RANK), lambda i: (i, ckv_blk)),
                  pl.BlockSpec((tm, LANES), lambda i: (i, kr_blk)),
                  pl.BlockSpec((tm, LANES), lambda i: (i, kr_blk + 1)),
                  pl.BlockSpec((1, MLA_KV_RANK), lambda i: (0, 0)),
                  pl.BlockSpec((tm, LANES), lambda i: (i, 0)),
                  pl.BlockSpec((1, LANES), lambda i: (0, 0)),
                  pl.BlockSpec((MLA_KV_RANK, n), lambda i: (0, 0))],
        out_specs=[pl.BlockSpec((tm, n), lambda i: (i, 0)),
                   pl.BlockSpec((tm, LANES), lambda i: (i, 0))],
        out_shape=[jax.ShapeDtypeStruct((m, n), BF16),
                   jax.ShapeDtypeStruct((m, LANES), BF16)],
        compiler_params=_params(("parallel",)),
        name="mla_kv_up",
    )(lat, lat, lat, g, posb, inv_row, w_ukv)


def _tri_tables(nblk):
    qt, kt = [], []
    for qi in range(nblk):
        for ki in range(qi + 1):
            qt.append(qi)
            kt.append(ki)
    return jnp.asarray(qt, jnp.int32), jnp.asarray(kt, jnp.int32)


def _softmax_step(s, v, c_blk, m_ref, l_ref, acc_ref):
    m_old = m_ref[...]
    m_new = jnp.maximum(m_old, jnp.max(s, axis=1, keepdims=True) + c_blk)
    alpha = jnp.exp(m_old - m_new)
    p = jnp.exp(s - (m_new - c_blk))
    l_ref[...] = alpha * l_ref[...] + jnp.sum(p, axis=1, keepdims=True)
    acc_ref[...] = alpha * acc_ref[...] + _dot(p.astype(v.dtype), v)
    m_ref[...] = m_new


def _causal_mask(t):
    row = lax.broadcasted_iota(jnp.int32, (t, t), 0)
    col = lax.broadcasted_iota(jnp.int32, (t, t), 1)
    return col <= row


def _da_kernel(qt_ref, kt_ref, slope_ref, q_ref, k_ref, v_ref, pk_ref, lam_ref, g_ref, o_ref,
               m_ref, l_ref, acc_ref, *, lambda_init):
    h = pl.program_id(1)
    p = pl.program_id(2)
    qi = qt_ref[p]
    ki = kt_ref[p]
    d = DA_HEAD_DIM
    t = q_ref.shape[0]

    @pl.when(ki == 0)
    def _():
        m_ref[...] = jnp.full(m_ref.shape, NEG_BIG, F32)
        l_ref[...] = jnp.zeros(l_ref.shape, F32)
        acc_ref[...] = jnp.zeros(acc_ref.shape, F32)

    def step(masked):
        slope = slope_ref[h]
        pk = pk_ref[...]
        pk0 = pk[:, :1]
        bias = slope * (pk - pk0)
        c_blk = slope * pk0
        v = v_ref[...]
        for mi in range(2):
            s = _dot_nt(q_ref[:, mi * d:(mi + 1) * d], k_ref[:, mi * d:(mi + 1) * d]) + bias
            if masked:
                s = jnp.where(_causal_mask(t), s, NEG_BIG)
            _softmax_step(s, v, c_blk, m_ref.at[mi], l_ref.at[mi], acc_ref.at[mi])

    @pl.when(ki < qi)
    def _():
        step(False)

    @pl.when(ki == qi)
    def _():
        step(True)
        lam_p = lam_ref[...]
        lam = (jnp.exp(jnp.sum(lam_p[0:1] * lam_p[1:2], axis=1, keepdims=True))
               - jnp.exp(jnp.sum(lam_p[2:3] * lam_p[3:4], axis=1, keepdims=True)) + lambda_init)
        o = acc_ref[0] / l_ref[0] - lam * (acc_ref[1] / l_ref[1])
        ms = jnp.mean(o * o, axis=-1, keepdims=True)
        o_ref[...] = (o * lax.rsqrt(ms + RMS_EPS) * g_ref[...] * (1.0 - lambda_init)).astype(o_ref.dtype)


def _diff_attention(qkv, pos_rows, lam_p, g, slopes, batch, seq, t, lambda_init):
    m = qkv.shape[0]
    nh = DA_HEADS
    nblk = seq // t
    qt, kt = _tri_tables(nblk)
    w = DA_V_DIM
    grid_spec = pltpu.PrefetchScalarGridSpec(
        num_scalar_prefetch=3,
        grid=(batch, nh, qt.shape[0]),
        in_specs=[pl.BlockSpec((t, w), lambda b, h, p, qt, kt, sl: (b * nblk + qt[p], h)),
                  pl.BlockSpec((t, w), lambda b, h, p, qt, kt, sl: (b * nblk + kt[p], nh + h)),
                  pl.BlockSpec((t, w), lambda b, h, p, qt, kt, sl: (b * nblk + kt[p], 2 * nh + h)),
                  pl.BlockSpec((None, 1, t), lambda b, h, p, qt, kt, sl: (b * nblk + kt[p], 0, 0)),
                  pl.BlockSpec((4, DA_HEAD_DIM), lambda b, h, p, qt, kt, sl: (0, 0)),
                  pl.BlockSpec((1, w), lambda b, h, p, qt, kt, sl: (0, 0))],
        out_specs=pl.BlockSpec((t, w), lambda b, h, p, qt, kt, sl: (b * nblk + qt[p], h)),
        scratch_shapes=[pltpu.VMEM((2, t, 1), F32),
                        pltpu.VMEM((2, t, 1), F32),
                        pltpu.VMEM((2, t, w), F32)],
    )
    return pl.pallas_call(
        functools.partial(_da_kernel, lambda_init=lambda_init),
        grid_spec=grid_spec,
        out_shape=jax.ShapeDtypeStruct((m, nh * w), BF16),
        compiler_params=_params(("parallel", "parallel", "arbitrary")),
        name="diff_attention",
    )(qt, kt, slopes, qkv, qkv, qkv, pos_rows, lam_p, g)


def _mla_kernel(qt_ref, kt_ref, q_ref, kv_ref, kr_ref, o_ref, m_ref, l_ref, acc_ref):
    p = pl.program_id(2)
    qi = qt_ref[p]
    ki = kt_ref[p]
    t = q_ref.shape[0]

    @pl.when(ki == 0)
    def _():
        m_ref[...] = jnp.full(m_ref.shape, NEG_BIG, F32)
        l_ref[...] = jnp.zeros(l_ref.shape, F32)
        acc_ref[...] = jnp.zeros(acc_ref.shape, F32)

    def step(masked):
        k_cat = jnp.concatenate([kv_ref[:, :MLA_NOPE_DIM], kr_ref[...]], axis=1)
        s = _dot_nt(q_ref[...], k_cat)
        if masked:
            s = jnp.where(_causal_mask(t), s, NEG_BIG)
        _softmax_step(s, kv_ref[:, MLA_NOPE_DIM:], 0.0, m_ref, l_ref, acc_ref)

    @pl.when(ki < qi)
    def _():
        step(False)

    @pl.when(ki == qi)
    def _():
        step(True)
        o_ref[...] = (acc_ref[...] / l_ref[...]).astype(o_ref.dtype)


def _mla_attention(qcat, kv, krope, batch, seq, t):
    m = qcat.shape[0]
    nh = MLA_HEADS
    nblk = seq // t
    qt, kt = _tri_tables(nblk)
    wq = 2 * LANES
    wkv = MLA_NOPE_DIM + MLA_V_DIM
    grid_spec = pltpu.PrefetchScalarGridSpec(
        num_scalar_prefetch=2,
        grid=(batch, nh, qt.shape[0]),
        in_specs=[pl.BlockSpec((t, wq), lambda b, h, p, qt, kt: (b * nblk + qt[p], h)),
                  pl.BlockSpec((t, wkv), lambda b, h, p, qt, kt: (b * nblk + kt[p], h)),
                  pl.BlockSpec((t, LANES), lambda b, h, p, qt, kt: (b * nblk + kt[p], 0))],
        out_specs=pl.BlockSpec((t, MLA_V_DIM), lambda b, h, p, qt, kt: (b * nblk + qt[p], h)),
        scratch_shapes=[pltpu.VMEM((t, 1), F32),
                        pltpu.VMEM((t, 1), F32),
                        pltpu.VMEM((t, MLA_V_DIM), F32)],
    )
    return pl.pallas_call(
        _mla_kernel,
        grid_spec=grid_spec,
        out_shape=jax.ShapeDtypeStruct((m, nh * MLA_V_DIM), BF16),
        compiler_params=_params(("parallel", "parallel", "arbitrary")),
        name="mla_attention",
    )(qt, kt, qcat, kv, krope)


def _merge_kernel(x_ref, oa_ref, ob_ref, wga_ref, wgb_ref, wpa_ref, wpb_ref, ba_ref, bb_ref, o_ref,
                  ga_ref, gb_ref, pa_ref, pb_ref):
    k = pl.program_id(2)
    x = x_ref[...]
    parts = ((ga_ref, _dot(x, wga_ref[...])), (gb_ref, _dot(x, wgb_ref[...])),
             (pa_ref, _dot(oa_ref[...], wpa_ref[...])), (pb_ref, _dot(ob_ref[...], wpb_ref[...])))

    @pl.when(k == 0)
    def _():
        for ref, val in parts:
            ref[...] = val

    @pl.when(k > 0)
    def _():
        for ref, val in parts:
            ref[...] += val

    @pl.when(k == pl.num_programs(2) - 1)
    def _():
        ga = jax.nn.sigmoid(ga_ref[...] + ba_ref[...])
        gb = jax.nn.sigmoid(gb_ref[...] + bb_ref[...])
        o_ref[...] = (ga * pa_ref[...] + gb * pb_ref[...]).astype(o_ref.dtype)


def _gated_merge(xb, oa, ob, wga, wgb, wpa, wpb, ba, bb, tm, tn, nk):
    m, d = xb.shape
    n = wga.shape[1]
    kx = d // nk
    ko = oa.shape[1] // nk
    return pl.pallas_call(
        _merge_kernel,
        grid=(m // tm, n // tn, nk),
        in_specs=[pl.BlockSpec((tm, kx), lambda i, j, k: (i, k)),
                  pl.BlockSpec((tm, ko), lambda i, j, k: (i, k)),
                  pl.BlockSpec((tm, ko), lambda i, j, k: (i, k)),
                  pl.BlockSpec((kx, tn), lambda i, j, k: (k, j)),
                  pl.BlockSpec((kx, tn), lambda i, j, k: (k, j)),
                  pl.BlockSpec((ko, tn), lambda i, j, k: (k, j)),
                  pl.BlockSpec((ko, tn), lambda i, j, k: (k, j)),
                  pl.BlockSpec((1, tn), lambda i, j, k: (0, j)),
                  pl.BlockSpec((1, tn), lambda i, j, k: (0, j))],
        out_specs=pl.BlockSpec((tm, tn), lambda i, j, k: (i, j)),
        out_shape=jax.ShapeDtypeStruct((m, n), BF16),
        scratch_shapes=[pltpu.VMEM((tm, tn), F32)] * 4,
        compiler_params=_params(("parallel", "parallel", "arbitrary")),
        name="gated_merge",
    )(xb, oa, ob, wga, wgb, wpa, wpb, ba, bb)


def _mm_res_kernel(a_ref, w_ref, r_ref, o_ref, *, alpha):
    o_ref[...] = alpha * r_ref[...] + _dot(a_ref[...], w_ref[...])


def _mm_res(a, w, res, alpha, tm, tn):
    m, k = a.shape
    n = w.shape[1]
    return pl.pallas_call(
        functools.partial(_mm_res_kernel, alpha=alpha),
        grid=(m // tm, n // tn),
        in_specs=[pl.BlockSpec((tm, k), lambda i, j: (i, 0)),
                  pl.BlockSpec((k, tn), lambda i, j: (0, j)),
                  pl.BlockSpec((tm, tn), lambda i, j: (i, j))],
        out_specs=pl.BlockSpec((tm, tn), lambda i, j: (i, j)),
        out_shape=jax.ShapeDtypeStruct((m, n), F32),
        compiler_params=_params(("parallel", "arbitrary")),
        name="out_proj_residual",
    )(a, w, res)


def _layernorm(y, g, b):
    mu = jnp.mean(y, axis=-1, keepdims=True)
    yc = y - mu
    var = jnp.mean(yc * yc, axis=-1, keepdims=True)
    return yc * lax.rsqrt(var + LN_EPS) * g + b


def _ffn_up_kernel(y_ref, g_ref, b_ref, w_ref, cw_ref, cb_ref, o_ref, h_ref, u_ref, carry_ref, *, tiles_per_seq):
    i = pl.program_id(0)
    j = pl.program_id(1)
    tm = y_ref.shape[0]
    half = o_ref.shape[1]
    pad = SUBLANES

    @pl.when(j == 0)
    def _():
        h_ref[...] = _layernorm(y_ref[...], g_ref[...], b_ref[...]).astype(h_ref.dtype)

    u_ref[pad:, :] = _dot(h_ref[...], w_ref[...])

    @pl.when(i % tiles_per_seq == 0)
    def _():
        u_ref[:pad, :] = jnp.zeros((pad, u_ref.shape[1]), F32)

    @pl.when(i % tiles_per_seq != 0)
    def _():
        u_ref[:pad, :] = carry_ref[j]

    cw = cw_ref[...]
    c = cb_ref[...] + cw[2:3] * u_ref[pad:, :]
    for back in range(1, CONV_WIDTH):
        c = c + cw[CONV_WIDTH - 1 - back:CONV_WIDTH - back] * u_ref[pad - back:pad - back + tm, :]
    carry_ref[j] = u_ref[tm:, :]
    gate = c[:, :half]
    o_ref[...] = (gate * jax.nn.sigmoid(gate) * c[:, half:]).astype(o_ref.dtype)


def _ffn_up(y, g, b, w_up, conv_w, conv_b, seq, tm, tn):
    m, d = y.shape
    n2 = w_up.shape[1]
    nj = n2 // (2 * tn)
    return pl.pallas_call(
        functools.partial(_ffn_up_kernel, tiles_per_seq=seq // tm),
        grid=(m // tm, nj),
        in_specs=[pl.BlockSpec((tm, d), lambda i, j: (i, 0)),
                  pl.BlockSpec((1, d), lambda i, j: (0, 0)),
                  pl.BlockSpec((1, d), lambda i, j: (0, 0)),
                  pl.BlockSpec((d, 2 * tn), lambda i, j: (0, j)),
                  pl.BlockSpec((CONV_WIDTH, 2 * tn), lambda i, j: (0, j)),
                  pl.BlockSpec((1, 2 * tn), lambda i, j: (0, j))],
        out_specs=pl.BlockSpec((tm, tn), lambda i, j: (i, j)),
        out_shape=jax.ShapeDtypeStruct((m, n2 // 2), BF16),
        scratch_shapes=[pltpu.VMEM((tm, d), BF16),
                        pltpu.VMEM((tm + SUBLANES, 2 * tn), F32),
                        pltpu.VMEM((nj, SUBLANES, 2 * tn), F32)],
        compiler_params=_params(("arbitrary", "arbitrary")),
        name="ffn_up_conv_gate",
    )(y, g, b, w_up, conv_w, conv_b)


def _ffn_down_kernel(a_ref, w_ref, y_ref, g1_ref, b1_ref, g2_ref, b2_ref, o_ref, *, alpha):
    k = pl.program_id(1)
    tm, d = o_ref.shape
    a = a_ref[...]
    for c in range(0, d, FFN_DOWN_COL_CHUNK):
        cols = slice(c, min(c + FFN_DOWN_COL_CHUNK, d))
        part = _dot(a, w_ref[:, cols])

        @pl.when(k == 0)
        def _():
            o_ref[:, cols] = part

        @pl.when(k > 0)
        def _():
            o_ref[:, cols] += part

    @pl.when(k == pl.num_programs(1) - 1)
    def _():
        rows_per = min(tm, FFN_DOWN_ROW_CHUNK)

        def body(r, carry):
            rows = pl.ds(pl.multiple_of(r * rows_per, rows_per), rows_per)
            h = _layernorm(y_ref[rows, :], g1_ref[...], b1_ref[...])
            o_ref[rows, :] = _layernorm(alpha * h + o_ref[rows, :], g2_ref[...], b2_ref[...])
            return carry

        lax.fori_loop(0, tm // rows_per, body, 0)


def _ffn_down(act, w_down, y, g1, b1, g2, b2, alpha, tm, tk):
    m, kdim = act.shape
    d = w_down.shape[1]
    vec = pl.BlockSpec((1, d), lambda i, k: (0, 0))
    return pl.pallas_call(
        functools.partial(_ffn_down_kernel, alpha=alpha),
        grid=(m // tm, kdim // tk),
        in_specs=[pl.BlockSpec((tm, tk), lambda i, k: (i, k)),
                  pl.BlockSpec((tk, d), lambda i, k: (k, 0)),
                  pl.BlockSpec((tm, d), lambda i, k: (i, 0)),
                  vec, vec, vec, vec],
        out_specs=pl.BlockSpec((tm, d), lambda i, k: (i, 0)),
        out_shape=jax.ShapeDtypeStruct((m, d), F32),
        compiler_params=_params(("parallel", "arbitrary")),
        name="ffn_down_layernorm",
    )(act, w_down, y, g1, b1, g2, b2)


def _rope_cols(w):
    half = MLA_ROPE_DIM // 2
    return jnp.concatenate([-w[:, half:], w[:, :half]], axis=1)


def _pad_cols(w, n):
    return jnp.pad(w, ((0, 0), (0, n - w.shape[1])))


def _layer(h, posb, pos_rows, batch, seq, alpha, lambda_init, w_in, b_gate, lam_p, da_g, q_g, kv_g, w_uq, w_ukv,
           w_proj_a, w_proj_b, w_out, ln1_g, ln1_b, w_up, conv_w, conv_b, w_down, ln2_g, ln2_b):
    m, d = h.shape
    f = w_down.shape[0]
    da_cols = 3 * DA_HEADS * DA_V_DIM
    off_cq = da_cols
    off_ckv = off_cq + MLA_Q_RANK
    off_kr = off_ckv + MLA_KV_RANK
    off_gate = off_kr + MLA_ROPE_DIM

    hb = h.astype(BF16)

    da_scale = DA_HEAD_DIM ** -0.5
    qkv_scale = jnp.concatenate([jnp.full((DA_HEADS * DA_V_DIM,), da_scale, F32),
                                 jnp.ones((2 * DA_HEADS * DA_V_DIM,), F32)])[None, :]
    qkv = _mm_scale(hb, w_in[:, :da_cols].astype(BF16), qkv_scale, BF16, _tile(m, 1024), 512, "da_qkv_proj")

    w_kr = w_in[:, off_kr:off_gate]
    w_lat = jnp.concatenate([w_in[:, off_cq:off_kr], _pad_cols(w_kr, LANES), _pad_cols(_rope_cols(w_kr), LANES)],
                            axis=1).astype(BF16)
    lat = _mm_scale(hb, w_lat, jnp.ones((1, w_lat.shape[1]), F32), F32, _tile(m, 1024), w_lat.shape[1] // 2,
                    "mla_latent_proj")

    slopes = 2.0 ** (-8.0 * jnp.arange(1, DA_HEADS + 1, dtype=F32) / DA_HEADS)
    t_attn = _tile(seq, 512)
    oa = _diff_attention(qkv, pos_rows, lam_p, da_g[None, :], slopes, batch, seq, t_attn, lambda_init)

    half = MLA_ROPE_DIM // 2
    inv = ROPE_THETA ** (-jnp.arange(half, dtype=F32) / half)
    inv_row = _pad_cols(jnp.concatenate([inv, inv])[None, :], LANES)
    hd = MLA_NOPE_DIM + MLA_ROPE_DIM
    wq = w_uq.reshape(MLA_Q_RANK, MLA_HEADS, hd)
    zeros = jnp.zeros((MLA_Q_RANK, MLA_HEADS, LANES - MLA_ROPE_DIM), F32)
    wq_main = jnp.concatenate([wq, zeros], axis=2).reshape(MLA_Q_RANK, MLA_HEADS * 2 * LANES).astype(BF16)
    wq_swap = jnp.concatenate([-wq[:, :, MLA_NOPE_DIM + half:], wq[:, :, MLA_NOPE_DIM:MLA_NOPE_DIM + half], zeros],
                              axis=2).reshape(MLA_Q_RANK, MLA_HEADS * LANES).astype(BF16)
    qcat = _q_up(lat, q_g[None, :], posb, inv_row, wq_main, wq_swap, hd ** -0.5, _tile(m, 1024))
    kv, krope = _kv_up(lat, kv_g[None, :], posb, inv_row, w_ukv.astype(BF16), _tile(m, 512))
    ob = _mla_attention(qcat, kv, krope, batch, seq, t_attn)

    merged = _gated_merge(hb, oa, ob, w_in[:, off_gate:off_gate + d].astype(BF16), w_in[:, off_gate + d:].astype(BF16),
                          w_proj_a.astype(BF16), w_proj_b.astype(BF16), b_gate[None, :d], b_gate[None, d:],
                          _tile(m, 1024), _tile(d, 512), 4)
    y1 = _mm_res(merged, w_out.astype(BF16), h, alpha, _tile(m, 1024), _tile(d, 512))

    tn = 512
    fp = -(-f // tn) * tn
    nj = fp // tn

    def group(a):
        r = a.shape[0]
        gv = jnp.stack([_pad_cols(a[:, :f], fp).reshape(r, nj, tn), _pad_cols(a[:, f:], fp).reshape(r, nj, tn)], axis=2)
        return gv.reshape(r, nj * 2 * tn)

    act = _ffn_up(y1, ln1_g[None, :], ln1_b[None, :], group(w_up).astype(BF16), group(conv_w), group(conv_b[None, :]),
                  seq, _tile(seq, 512), tn)
    w_down_p = jnp.pad(w_down, ((0, fp - f), (0, 0))).astype(BF16)
    return _ffn_down(act, w_down_p, y1, ln1_g[None, :], ln1_b[None, :], ln2_g[None, :], ln2_b[None, :], alpha,
                     _tile(m, 512), 512)


def kernel(x, positions, w_in, b_gate, da_lambda_q1, da_lambda_k1, da_lambda_q2, da_lambda_k2, da_subln_g, mla_q_norm_g, mla_kv_norm_g, w_uq, w_ukv, w_proj_a, w_proj_b, w_out, ln1_g, ln1_b, w_up, conv_w, conv_b, w_down, ln2_g, ln2_b):
    batch, seq, d = x.shape
    depth = w_in.shape[0]
    alpha = (2.0 * depth) ** 0.25
    m = batch * seq
    posf = positions.astype(F32).reshape(m)
    posb = jnp.broadcast_to(posf[:, None], (m, LANES))
    t_attn = _tile(seq, 512)
    pos_rows = posf.reshape(m // t_attn, 1, t_attn)
    h = x.reshape(m, d)
    for l in range(depth):
        lambda_init = 0.8 - 0.6 * math.exp(-0.3 * l)
        lam_p = jnp.stack([da_lambda_q1[l], da_lambda_k1[l], da_lambda_q2[l], da_lambda_k2[l]]).astype(F32)
        h = _layer(h, posb, pos_rows, batch, seq, alpha, lambda_init, w_in[l], b_gate[l], lam_p, da_subln_g[l],
                   mla_q_norm_g[l], mla_kv_norm_g[l], w_uq[l], w_ukv[l], w_proj_a[l], w_proj_b[l], w_out[l],
                   ln1_g[l], ln1_b[l], w_up[l], conv_w[l], conv_b[l], w_down[l], ln2_g[l], ln2_b[l])
    return h.reshape(batch, seq, d)
```

```python
import functools
import math

import numpy as np
import jax
import jax.numpy as jnp
from jax import lax
from jax.experimental import pallas as pl
from jax.experimental.pallas import tpu as pltpu

DA_HEADS = 8
DA_HEAD_DIM = 128
DA_V_DIM = 2 * DA_HEAD_DIM
MLA_HEADS = 16
MLA_Q_RANK = 1024
MLA_KV_RANK = 512
MLA_NOPE_DIM = 128
MLA_ROPE_DIM = 64
MLA_V_DIM = 128
ROPE_THETA = 10000.0
CONV_WIDTH = 3
LN_EPS = 1e-5
RMS_EPS = 1e-6

LANES = 128
SUBLANES = 8
VMEM_LIMIT = 56 * 1024 * 1024
NEG_BIG = -1e30
LOG2E = math.log2(math.e)
FFN_DOWN_COL_CHUNK = 1024
FFN_DOWN_ROW_CHUNK = 64

F32 = jnp.float32
BF16 = jnp.bfloat16


def _params(sem):
    return pltpu.CompilerParams(dimension_semantics=sem, vmem_limit_bytes=VMEM_LIMIT)


def _tile(n, pref):
    t = min(n, pref)
    while n % t:
        t //= 2
    return t


def _dot(a, b):
    return jnp.dot(a, b, preferred_element_type=F32)


def _dot_nt(a, b):
    return lax.dot_general(a, b, (((1,), (1,)), ((), ())), preferred_element_type=F32)


def _mm_scale_kernel(a_ref, w_ref, s_ref, o_ref):
    o_ref[...] = (_dot(a_ref[...], w_ref[...]) * s_ref[...]).astype(o_ref.dtype)


def _mm_scale(a, w, scale, out_dtype, tm, tn, name):
    m, k = a.shape
    n = w.shape[1]
    return pl.pallas_call(
        _mm_scale_kernel,
        grid=(m // tm, n // tn),
        in_specs=[pl.BlockSpec((tm, k), lambda i, j: (i, 0)),
                  pl.BlockSpec((k, tn), lambda i, j: (0, j)),
                  pl.BlockSpec((1, tn), lambda i, j: (0, j))],
        out_specs=pl.BlockSpec((tm, tn), lambda i, j: (i, j)),
        out_shape=jax.ShapeDtypeStruct((m, n), out_dtype),
        compiler_params=_params(("parallel", "arbitrary")),
        name=name,
    )(a, w, scale)


def _rms_bf16(c, g):
    ms = jnp.mean(c * c, axis=-1, keepdims=True)
    return (c * lax.rsqrt(ms + RMS_EPS) * g).astype(BF16)


def _qup_kernel(cq_ref, g_ref, pos_ref, inv_ref, wm_ref, ws_ref, o_ref, cn_ref, cos_ref, sin_ref, *, scale):
    @pl.when(pl.program_id(1) == 0)
    def _():
        cn_ref[...] = _rms_bf16(cq_ref[...], g_ref[...])
        ang = pos_ref[...] * inv_ref[...]
        cos_ref[...] = jnp.cos(ang)
        sin_ref[...] = jnp.sin(ang)

    cn = cn_ref[...]
    main = _dot(cn, wm_ref[...])
    swapped = _dot(cn, ws_ref[...])
    rope = main[:, LANES:] * cos_ref[...] + swapped * sin_ref[...]
    o_ref[:, :LANES] = (main[:, :LANES] * scale).astype(o_ref.dtype)
    o_ref[:, LANES:] = (rope * scale).astype(o_ref.dtype)


def _q_up(lat, g, posb, inv_row, w_main, w_swap, scale, tm):
    m = lat.shape[0]
    nh = MLA_HEADS
    return pl.pallas_call(
        functools.partial(_qup_kernel, scale=scale),
        grid=(m // tm, nh),
        in_specs=[pl.BlockSpec((tm, MLA_Q_RANK), lambda i, h: (i, 0)),
                  pl.BlockSpec((1, MLA_Q_RANK), lambda i, h: (0, 0)),
                  pl.BlockSpec((tm, LANES), lambda i, h: (i, 0)),
                  pl.BlockSpec((1, LANES), lambda i, h: (0, 0)),
                  pl.BlockSpec((MLA_Q_RANK, 2 * LANES), lambda i, h: (0, h)),
                  pl.BlockSpec((MLA_Q_RANK, LANES), lambda i, h: (0, h))],
        out_specs=pl.BlockSpec((tm, 2 * LANES), lambda i, h: (i, h)),
        out_shape=jax.ShapeDtypeStruct((m, nh * 2 * LANES), BF16),
        scratch_shapes=[pltpu.VMEM((tm, MLA_Q_RANK), BF16),
                        pltpu.VMEM((tm, LANES), F32),
                        pltpu.VMEM((tm, LANES), F32)],
        compiler_params=_params(("parallel", "arbitrary")),
        name="mla_q_up",
    )(lat, g, posb, inv_row, w_main, w_swap)


def _kvup_kernel(ckv_ref, kr_ref, krs_ref, g_ref, pos_ref, inv_ref, w_ref, k_ref, v_ref):
    tm = ckv_ref.shape[0]
    cn = _rms_bf16(ckv_ref[...], g_ref[...])
    ang = pos_ref[...] * inv_ref[...]
    krope = (kr_ref[...] * jnp.cos(ang) + krs_ref[...] * jnp.sin(ang)).astype(k_ref.dtype)
    ones_col = (lax.broadcasted_iota(jnp.int32, (tm, LANES), 1) == 0).astype(v_ref.dtype)
    wh = MLA_NOPE_DIM + MLA_V_DIM
    for h in range(MLA_HEADS):
        kv = _dot(cn, w_ref[:, h * wh:(h + 1) * wh])
        k_ref[:, h * wh:h * wh + MLA_NOPE_DIM] = kv[:, :MLA_NOPE_DIM].astype(k_ref.dtype)
        k_ref[:, h * wh + MLA_NOPE_DIM:(h + 1) * wh] = krope
        v_ref[:, h * wh:h * wh + MLA_V_DIM] = kv[:, MLA_NOPE_DIM:].astype(v_ref.dtype)
        v_ref[:, h * wh + MLA_V_DIM:(h + 1) * wh] = ones_col


def _kv_up(lat, g, posb, inv_row, w_ukv, tm):
    m = lat.shape[0]
    n = w_ukv.shape[1]
    ckv_blk = MLA_Q_RANK // MLA_KV_RANK
    kr_blk = (MLA_Q_RANK + MLA_KV_RANK) // LANES
    return pl.pallas_call(
        _kvup_kernel,
        grid=(m // tm,),
        in_specs=[pl.BlockSpec((tm, MLA_KV_RANK), lambda i: (i, ckv_blk)),
                  pl.BlockSpec((tm, LANES), lambda i: (i, kr_blk)),
                  pl.BlockSpec((tm, LANES), lambda i: (i, kr_blk + 1)),
                  pl.BlockSpec((1, MLA_KV_RANK), lambda i: (0, 0)),
                  pl.BlockSpec((tm, LANES), lambda i: (i, 0)),
                  pl.BlockSpec((1, LANES), lambda i: (0, 0)),
                  pl.BlockSpec((MLA_KV_RANK, n), lambda i: (0, 0))],
        out_specs=[pl.BlockSpec((tm, n), lambda i: (i, 0)),
                   pl.BlockSpec((tm, n), lambda i: (i, 0))],
        out_shape=[jax.ShapeDtypeStruct((m, n), BF16),
                   jax.ShapeDtypeStruct((m, n), BF16)],
        compiler_params=_params(("parallel",)),
        name="mla_kv_up",
    )(lat, lat, lat, g, posb, inv_row, w_ukv)


def _row_max(s):
    part = s[:, :LANES]
    for c in range(LANES, s.shape[1], LANES):
        part = jnp.maximum(part, s[:, c:c + LANES])
    return jnp.max(part, axis=1, keepdims=True)


def _lane_partial_sum(p):
    part = p[:, :LANES]
    for c in range(LANES, p.shape[1], LANES):
        part = part + p[:, c:c + LANES]
    return part


def _causal_mask(rows, cols, offset):
    row = lax.broadcasted_iota(jnp.int32, (rows, cols), 0)
    col = lax.broadcasted_iota(jnp.int32, (rows, cols), 1)
    return col <= row + offset


def _da_kernel(slope_ref, q_ref, k_ref, v_ref, pos_ref, lam_ref, g_ref, o_ref, m_ref, l_ref, acc_ref, *, lambda_init):
    h = pl.program_id(1)
    qi = pl.program_id(2)
    d = DA_HEAD_DIM
    tq = q_ref.shape[0]
    tk = 2 * tq

    m_ref[...] = jnp.full(m_ref.shape, NEG_BIG, F32)
    l_ref[...] = jnp.zeros(l_ref.shape, F32)
    acc_ref[...] = jnp.zeros(acc_ref.shape, F32)

    slope2 = slope_ref[h] * LOG2E
    q0 = pl.multiple_of(qi * tq, tq)
    pq0 = pos_ref[:, pl.ds(q0, LANES)][:, :1]

    def update(start, width, offset):
        bias = slope2 * (pos_ref[:, pl.ds(start, width)] - pq0)
        v = v_ref[pl.ds(start, width), :]
        for mi in range(2):
            cols = slice(mi * d, (mi + 1) * d)
            s = _dot_nt(q_ref[:, cols], k_ref[pl.ds(start, width), cols]) + bias
            if offset is not None:
                s = jnp.where(_causal_mask(tq, width, offset), s, NEG_BIG)
            m_old = m_ref[mi]
            m_new = jnp.maximum(m_old, _row_max(s))
            alpha = jnp.exp2(m_old - m_new)
            p = jnp.exp2(s - m_new)
            l_ref[mi] = alpha * l_ref[mi] + _lane_partial_sum(p)
            acc_ref[mi] = alpha * acc_ref[mi] + _dot(p.astype(v.dtype), v)
            m_ref[mi] = m_new

    def full_chunk(c, carry):
        update(pl.multiple_of(c * tk, tk), tk, None)
        return carry

    lax.fori_loop(0, qi // 2, full_chunk, 0)

    @pl.when(qi % 2 == 0)
    def _():
        update(q0, tq, 0)

    @pl.when(qi % 2 == 1)
    def _():
        update(pl.multiple_of(q0 - tq, tq), tk, tq)

    lam_p = lam_ref[...]
    lam = (jnp.exp(jnp.sum(lam_p[0:1] * lam_p[1:2], axis=1, keepdims=True))
           - jnp.exp(jnp.sum(lam_p[2:3] * lam_p[3:4], axis=1, keepdims=True)) + lambda_init)
    l0 = jnp.sum(l_ref[0], axis=1, keepdims=True)
    l1 = jnp.sum(l_ref[1], axis=1, keepdims=True)
    o = acc_ref[0] / l0 - lam * (acc_ref[1] / l1)
    ms = jnp.mean(o * o, axis=-1, keepdims=True)
    o_ref[...] = (o * lax.rsqrt(ms + RMS_EPS) * g_ref[...] * (1.0 - lambda_init)).astype(o_ref.dtype)


def _diff_attention(qkv, pos_rows, lam_p, g, slopes, batch, seq, tq, lambda_init):
    m = qkv.shape[0]
    nh = DA_HEADS
    nq = seq // tq
    w = DA_V_DIM
    grid_spec = pltpu.PrefetchScalarGridSpec(
        num_scalar_prefetch=1,
        grid=(batch, nh, nq),
        in_specs=[pl.BlockSpec((tq, w), lambda b, h, qi, sl: (b * nq + qi, h)),
                  pl.BlockSpec((seq, w), lambda b, h, qi, sl: (b, nh + h)),
                  pl.BlockSpec((seq, w), lambda b, h, qi, sl: (b, 2 * nh + h)),
                  pl.BlockSpec((None, 1, seq), lambda b, h, qi, sl: (b, 0, 0)),
                  pl.BlockSpec((4, DA_HEAD_DIM), lambda b, h, qi, sl: (0, 0)),
                  pl.BlockSpec((1, w), lambda b, h, qi, sl: (0, 0))],
        out_specs=pl.BlockSpec((tq, w), lambda b, h, qi, sl: (b * nq + qi, h)),
        scratch_shapes=[pltpu.VMEM((2, tq, 1), F32),
                        pltpu.VMEM((2, tq, LANES), F32),
                        pltpu.VMEM((2, tq, w), F32)],
    )
    return pl.pallas_call(
        functools.partial(_da_kernel, lambda_init=lambda_init),
        grid_spec=grid_spec,
        out_shape=jax.ShapeDtypeStruct((m, nh * w), BF16),
        compiler_params=_params(("parallel", "parallel", "arbitrary")),
        name="diff_attention",
    )(slopes, qkv, qkv, qkv, pos_rows, lam_p, g)


def _mla_kernel(q_ref, k_ref, v_ref, o_ref, m_ref, acc_ref):
    qi = pl.program_id(2)
    tq = q_ref.shape[0]
    half = tq // 2

    m_ref[...] = jnp.full(m_ref.shape, NEG_BIG, F32)
    acc_ref[...] = jnp.zeros(acc_ref.shape, F32)

    def update(r0, start, width, offset):
        rows = slice(r0, r0 + half)
        s = _dot_nt(q_ref[rows, :], k_ref[pl.ds(start, width), :])
        if offset is not None:
            s = jnp.where(_causal_mask(half, width, offset), s, NEG_BIG)
        m_old = m_ref[rows, :]
        m_new = jnp.maximum(m_old, _row_max(s))
        alpha = jnp.exp2(m_old - m_new)
        p = jnp.exp2(s - m_new).astype(v_ref.dtype)
        acc_ref[rows, :] = alpha * acc_ref[rows, :] + _dot(p, v_ref[pl.ds(start, width), :])
        m_ref[rows, :] = m_new

    def full_chunk(c, carry):
        start = pl.multiple_of(c * tq, tq)
        update(0, start, tq, None)
        update(half, start, tq, None)
        return carry

    lax.fori_loop(0, qi, full_chunk, 0)
    d0 = pl.multiple_of(qi * tq, tq)
    update(0, d0, half, 0)
    update(half, d0, tq, half)
    acc = acc_ref[...]
    o_ref[...] = (acc[:, :MLA_V_DIM] / acc[:, MLA_V_DIM:MLA_V_DIM + 1]).astype(o_ref.dtype)


def _mla_attention(qcat, kcat, vext, batch, seq, tq):
    m = qcat.shape[0]
    nh = MLA_HEADS
    nq = seq // tq
    w = 2 * LANES
    return pl.pallas_call(
        _mla_kernel,
        grid=(batch, nh, nq),
        in_specs=[pl.BlockSpec((tq, w), lambda b, h, qi: (b * nq + qi, h)),
                  pl.BlockSpec((seq, w), lambda b, h, qi: (b, h)),
                  pl.BlockSpec((seq, w), lambda b, h, qi: (b, h))],
        out_specs=pl.BlockSpec((tq, MLA_V_DIM), lambda b, h, qi: (b * nq + qi, h)),
        out_shape=jax.ShapeDtypeStruct((m, nh * MLA_V_DIM), BF16),
        scratch_shapes=[pltpu.VMEM((tq, 1), F32),
                        pltpu.VMEM((tq, w), F32)],
        compiler_params=_params(("parallel", "parallel", "arbitrary")),
        name="mla_attention",
    )(qcat, kcat, vext)


def _merge_kernel(x_ref, oa_ref, ob_ref, wga_ref, wgb_ref, wpa_ref, wpb_ref, ba_ref, bb_ref, o_ref):
    d = x_ref.shape[1]
    kc = oa_ref.shape[1] if d % oa_ref.shape[1] == 0 else d

    def gate(w_ref, b_ref):
        acc = b_ref[...] + _dot(x_ref[:, :kc], w_ref[:kc, :])
        for c in range(kc, d, kc):
            acc = acc + _dot(x_ref[:, c:c + kc], w_ref[c:c + kc, :])
        return jax.nn.sigmoid(acc)

    ga = gate(wga_ref, ba_ref)
    gb = gate(wgb_ref, bb_ref)
    pa = _dot(oa_ref[...], wpa_ref[...])
    pb = _dot(ob_ref[...], wpb_ref[...])
    o_ref[...] = (ga * pa + gb * pb).astype(o_ref.dtype)


def _gated_merge(xb, oa, ob, wga, wgb, wpa, wpb, ba, bb, tm, tn):
    m, d = xb.shape
    n = wga.shape[1]
    ko = oa.shape[1]
    return pl.pallas_call(
        _merge_kernel,
        grid=(m // tm, n // tn),
        in_specs=[pl.BlockSpec((tm, d), lambda i, j: (i, 0)),
                  pl.BlockSpec((tm, ko), lambda i, j: (i, 0)),
                  pl.BlockSpec((tm, ko), lambda i, j: (i, 0)),
                  pl.BlockSpec((d, tn), lambda i, j: (0, j)),
                  pl.BlockSpec((d, tn), lambda i, j: (0, j)),
                  pl.BlockSpec((ko, tn), lambda i, j: (0, j)),
                  pl.BlockSpec((ko, tn), lambda i, j: (0, j)),
                  pl.BlockSpec((1, tn), lambda i, j: (0, j)),
                  pl.BlockSpec((1, tn), lambda i, j: (0, j))],
        out_specs=pl.BlockSpec((tm, tn), lambda i, j: (i, j)),
        out_shape=jax.ShapeDtypeStruct((m, n), BF16),
        compiler_params=_params(("parallel", "arbitrary")),
        name="gated_merge",
    )(xb, oa, ob, wga, wgb, wpa, wpb, ba, bb)


def _mm_res_kernel(a_ref, w_ref, r_ref, o_ref, *, alpha):
    o_ref[...] = alpha * r_ref[...] + _dot(a_ref[...], w_ref[...])


def _mm_res(a, w, res, alpha, tm, tn):
    m, k = a.shape
    n = w.shape[1]
    return pl.pallas_call(
        functools.partial(_mm_res_kernel, alpha=alpha),
        grid=(m // tm, n // tn),
        in_specs=[pl.BlockSpec((tm, k), lambda i, j: (i, 0)),
                  pl.BlockSpec((k, tn), lambda i, j: (0, j)),
                  pl.BlockSpec((tm, tn), lambda i, j: (i, j))],
        out_specs=pl.BlockSpec((tm, tn), lambda i, j: (i, j)),
        out_shape=jax.ShapeDtypeStruct((m, n), F32),
        compiler_params=_params(("parallel", "arbitrary")),
        name="out_proj_residual",
    )(a, w, res)


def _layernorm(y, g, b):
    mu = jnp.mean(y, axis=-1, keepdims=True)
    yc = y - mu
    var = jnp.mean(yc * yc, axis=-1, keepdims=True)
    return yc * lax.rsqrt(var + LN_EPS) * g + b


def _ffn_up_kernel(y_ref, g_ref, b_ref, w_ref, cw_ref, cb_ref, o_ref, h_ref, u_ref, carry_ref, *, nj, tiles_per_seq):
    t = pl.program_id(0)
    last = pl.num_programs(0) - 1
    tm = y_ref.shape[0]
    half = o_ref.shape[1]
    pad = SUBLANES

    @pl.when(t == 0)
    def _():
        u_ref[...] = jnp.zeros(u_ref.shape, F32)
        carry_ref[...] = jnp.zeros(carry_ref.shape, F32)

    @pl.when((t % nj == 0) & (t < last))
    def _():
        h_ref[...] = _layernorm(y_ref[...], g_ref[...], b_ref[...]).astype(h_ref.dtype)

    tp = jnp.maximum(t - 1, 0)
    ip = tp // nj
    jp = tp % nj
    seq_start = (ip % tiles_per_seq == 0).astype(F32)
    u_ref[:pad, :] = carry_ref[jp] * (1.0 - seq_start)
    cw = cw_ref[...]
    c = cb_ref[...] + cw[2:3] * u_ref[pad:, :]
    for back in range(1, CONV_WIDTH):
        c = c + cw[CONV_WIDTH - 1 - back:CONV_WIDTH - back] * u_ref[pad - back:pad - back + tm, :]
    carry_ref[jp] = u_ref[tm:, :]
    gate = c[:, :half]
    o_ref[...] = (gate * jax.nn.sigmoid(gate) * c[:, half:]).astype(o_ref.dtype)

    u_ref[pad:, :] = _dot(h_ref[...], w_ref[...])


def _ffn_up(y, g, b, w_up, conv_w, conv_b, seq, tm, tn):
    m, d = y.shape
    n2 = w_up.shape[1]
    nj = n2 // (2 * tn)
    steps = (m // tm) * nj

    def cur(t):
        return jnp.minimum(t, steps - 1)

    def prev(t):
        return jnp.maximum(t - 1, 0)

    return pl.pallas_call(
        functools.partial(_ffn_up_kernel, nj=nj, tiles_per_seq=seq // tm),
        grid=(steps + 1,),
        in_specs=[pl.BlockSpec((tm, d), lambda t: (cur(t) // nj, 0)),
                  pl.BlockSpec((1, d), lambda t: (0, 0)),
                  pl.BlockSpec((1, d), lambda t: (0, 0)),
                  pl.BlockSpec((d, 2 * tn), lambda t: (0, cur(t) % nj)),
                  pl.BlockSpec((CONV_WIDTH, 2 * tn), lambda t: (0, prev(t) % nj)),
                  pl.BlockSpec((1, 2 * tn), lambda t: (0, prev(t) % nj))],
        out_specs=pl.BlockSpec((tm, tn), lambda t: (prev(t) // nj, prev(t) % nj)),
        out_shape=jax.ShapeDtypeStruct((m, n2 // 2), BF16),
        scratch_shapes=[pltpu.VMEM((tm, d), BF16),
                        pltpu.VMEM((tm + SUBLANES, 2 * tn), F32),
                        pltpu.VMEM((nj, SUBLANES, 2 * tn), F32)],
        compiler_params=_params(("arbitrary",)),
        name="ffn_up_conv_gate",
    )(y, g, b, w_up, conv_w, conv_b)


def _ffn_down_kernel(a_ref, w_ref, y_ref, g1_ref, b1_ref, g2_ref, b2_ref, o_ref, *, alpha):
    k = pl.program_id(1)
    tm, d = o_ref.shape

    @pl.when(k == 0)
    def _():
        o_ref[...] = jnp.zeros(o_ref.shape, F32)

    a = a_ref[...]
    for c in range(0, d, FFN_DOWN_COL_CHUNK):
        cols = slice(c, min(c + FFN_DOWN_COL_CHUNK, d))
        o_ref[:, cols] += _dot(a, w_ref[:, cols])

    @pl.when(k == pl.num_programs(1) - 1)
    def _():
        rows_per = min(tm, FFN_DOWN_ROW_CHUNK)

        def body(r, carry):
            rows = pl.ds(pl.multiple_of(r * rows_per, rows_per), rows_per)
            h = _layernorm(y_ref[rows, :], g1_ref[...], b1_ref[...])
            o_ref[rows, :] = _layernorm(alpha * h + o_ref[rows, :], g2_ref[...], b2_ref[...])
            return carry

        lax.fori_loop(0, tm // rows_per, body, 0)


def _ffn_down(act, w_down, y, g1, b1, g2, b2, alpha, tm, tk):
    m, kdim = act.shape
    d = w_down.shape[1]
    vec = pl.BlockSpec((1, d), lambda i, k: (0, 0))
    return pl.pallas_call(
        functools.partial(_ffn_down_kernel, alpha=alpha),
        grid=(m // tm, kdim // tk),
        in_specs=[pl.BlockSpec((tm, tk), lambda i, k: (i, k)),
                  pl.BlockSpec((tk, d), lambda i, k: (k, 0)),
                  pl.BlockSpec((tm, d), lambda i, k: (i, 0)),
                  vec, vec, vec, vec],
        out_specs=pl.BlockSpec((tm, d), lambda i, k: (i, 0)),
        out_shape=jax.ShapeDtypeStruct((m, d), F32),
        compiler_params=_params(("parallel", "arbitrary")),
        name="ffn_down_layernorm",
    )(act, w_down, y, g1, b1, g2, b2)


def _rope_cols(w):
    half = MLA_ROPE_DIM // 2
    return jnp.concatenate([-w[:, half:], w[:, :half]], axis=1)


def _pad_cols(w, n):
    return jnp.pad(w, ((0, 0), (0, n - w.shape[1])))


def _layer(h, posb, pos_rows, batch, seq, alpha, lambda_init, w_in, b_gate, lam_p, da_g, q_g, kv_g, w_uq, w_ukv,
           w_proj_a, w_proj_b, w_out, ln1_g, ln1_b, w_up, conv_w, conv_b, w_down, ln2_g, ln2_b):
    m, d = h.shape
    f = w_down.shape[0]
    da_cols = 3 * DA_HEADS * DA_V_DIM
    off_cq = da_cols
    off_ckv = off_cq + MLA_Q_RANK
    off_kr = off_ckv + MLA_KV_RANK
    off_gate = off_kr + MLA_ROPE_DIM

    hb = h.astype(BF16)

    da_scale = DA_HEAD_DIM ** -0.5 * LOG2E
    qkv_scale = jnp.concatenate([jnp.full((DA_HEADS * DA_V_DIM,), da_scale, F32),
                                 jnp.ones((2 * DA_HEADS * DA_V_DIM,), F32)])[None, :]
    qkv = _mm_scale(hb, w_in[:, :da_cols].astype(BF16), qkv_scale, BF16, _tile(m, 1024), 512, "da_qkv_proj")

    w_kr = w_in[:, off_kr:off_gate]
    w_lat = jnp.concatenate([w_in[:, off_cq:off_kr], _pad_cols(w_kr, LANES), _pad_cols(_rope_cols(w_kr), LANES)],
                            axis=1).astype(BF16)
    lat = _mm_scale(hb, w_lat, jnp.ones((1, w_lat.shape[1]), F32), F32, _tile(m, 1024), w_lat.shape[1] // 2,
                    "mla_latent_proj")

    slopes = 2.0 ** (-8.0 * jnp.arange(1, DA_HEADS + 1, dtype=F32) / DA_HEADS)
    oa = _diff_attention(qkv, pos_rows, lam_p, da_g[None, :], slopes, batch, seq, _tile(seq, 512), lambda_init)

    half = MLA_ROPE_DIM // 2
    inv = ROPE_THETA ** (-jnp.arange(half, dtype=F32) / half)
    inv_row = _pad_cols(jnp.concatenate([inv, inv])[None, :], LANES)
    hd = MLA_NOPE_DIM + MLA_ROPE_DIM
    wq = w_uq.reshape(MLA_Q_RANK, MLA_HEADS, hd)
    zeros = jnp.zeros((MLA_Q_RANK, MLA_HEADS, LANES - MLA_ROPE_DIM), F32)
    wq_main = jnp.concatenate([wq, zeros], axis=2).reshape(MLA_Q_RANK, MLA_HEADS * 2 * LANES).astype(BF16)
    wq_swap = jnp.concatenate([-wq[:, :, MLA_NOPE_DIM + half:], wq[:, :, MLA_NOPE_DIM:MLA_NOPE_DIM + half], zeros],
                              axis=2).reshape(MLA_Q_RANK, MLA_HEADS * LANES).astype(BF16)
    qcat = _q_up(lat, q_g[None, :], posb, inv_row, wq_main, wq_swap, hd ** -0.5 * LOG2E, _tile(m, 1024))
    kcat, vext = _kv_up(lat, kv_g[None, :], posb, inv_row, w_ukv.astype(BF16), _tile(m, 512))
    ob = _mla_attention(qcat, kcat, vext, batch, seq, _tile(seq, 1024))

    merged = _gated_merge(hb, oa, ob, w_in[:, off_gate:off_gate + d].astype(BF16), w_in[:, off_gate + d:].astype(BF16),
                          w_proj_a.astype(BF16), w_proj_b.astype(BF16), b_gate[None, :d], b_gate[None, d:],
                          _tile(m, 512), _tile(d, 512))
    y1 = _mm_res(merged, w_out.astype(BF16), h, alpha, _tile(m, 1024), _tile(d, 512))

    tn = 512
    fp = -(-f // tn) * tn
    nj = fp // tn

    def group(a):
        r = a.shape[0]
        gv = jnp.stack([_pad_cols(a[:, :f], fp).reshape(r, nj, tn), _pad_cols(a[:, f:], fp).reshape(r, nj, tn)], axis=2)
        return gv.reshape(r, nj * 2 * tn)

    act = _ffn_up(y1, ln1_g[None, :], ln1_b[None, :], group(w_up.astype(BF16)), group(conv_w), group(conv_b[None, :]),
                  seq, _tile(seq, 512), tn)
    w_down_p = jnp.pad(w_down.astype(BF16), ((0, fp - f), (0, 0)))
    return _ffn_down(act, w_down_p, y1, ln1_g[None, :], ln1_b[None, :], ln2_g[None, :], ln2_b[None, :], alpha,
                     _tile(m, 512), 512)


def kernel(x, positions, w_in, b_gate, da_lambda_q1, da_lambda_k1, da_lambda_q2, da_lambda_k2, da_subln_g, mla_q_norm_g, mla_kv_norm_g, w_uq, w_ukv, w_proj_a, w_proj_b, w_out, ln1_g, ln1_b, w_up, conv_w, conv_b, w_down, ln2_g, ln2_b):
    batch, seq, d = x.shape
    depth = w_in.shape[0]
    alpha = (2.0 * depth) ** 0.25
    m = batch * seq
    posf = positions.astype(F32).reshape(m)
    posb = jnp.broadcast_to(posf[:, None], (m, LANES))
    pos_rows = posf.reshape(batch, 1, seq)
    h = x.reshape(m, d)
    for l in range(depth):
        lambda_init = 0.8 - 0.6 * math.exp(-0.3 * l)
        lam_p = jnp.stack([da_lambda_q1[l], da_lambda_k1[l], da_lambda_q2[l], da_lambda_k2[l]]).astype(F32)
        h = _layer(h, posb, pos_rows, batch, seq, alpha, lambda_init, w_in[l], b_gate[l], lam_p, da_subln_g[l],
                   mla_q_norm_g[l], mla_kv_norm_g[l], w_uq[l], w_ukv[l], w_proj_a[l], w_proj_b[l], w_out[l],
                   ln1_g[l], ln1_b[l], w_up[l], conv_w[l], conv_b[l], w_down[l], ln2_g[l], ln2_b[l])
    return h.reshape(batch, seq, d)
```

```python
import functools
import math

import numpy as np
import jax
import jax.numpy as jnp
from jax import lax
from jax.experimental import pallas as pl
from jax.experimental.pallas import tpu as pltpu

DA_HEADS = 8
DA_HEAD_DIM = 128
DA_V_DIM = 2 * DA_HEAD_DIM
MLA_HEADS = 16
MLA_Q_RANK = 1024
MLA_KV_RANK = 512
MLA_NOPE_DIM = 128
MLA_ROPE_DIM = 64
MLA_V_DIM = 128
ROPE_THETA = 10000.0
CONV_WIDTH = 3
LN_EPS = 1e-5
RMS_EPS = 1e-6

LANES = 128
SUBLANES = 8
VMEM_LIMIT = 56 * 1024 * 1024
NEG_BIG = -1e30
LOG2E = math.log2(math.e)
FFN_DOWN_COL_CHUNK = 1024
FFN_DOWN_ROW_CHUNK = 64

F32 = jnp.float32
BF16 = jnp.bfloat16


def _params(sem):
    return pltpu.CompilerParams(dimension_semantics=sem, vmem_limit_bytes=VMEM_LIMIT)


def _tile(n, pref):
    t = min(n, pref)
    while n % t:
        t //= 2
    return t


def _dot(a, b):
    return jnp.dot(a, b, preferred_element_type=F32)


def _dot_nt(a, b):
    return lax.dot_general(a, b, (((1,), (1,)), ((), ())), preferred_element_type=F32)


def _mm_scale_kernel(a_ref, wt_ref, s_ref, o_ref):
    o_ref[...] = (_dot_nt(a_ref[...], wt_ref[...]) * s_ref[...]).astype(o_ref.dtype)


def _mm_scale(a, wt, scale, out_dtype, tm, tn, name):
    m, k = a.shape
    n = scale.shape[1]
    return pl.pallas_call(
        _mm_scale_kernel,
        grid=(m // tm, n // tn),
        in_specs=[pl.BlockSpec((tm, k), lambda i, j: (i, 0)),
                  pl.BlockSpec((tn, k), lambda i, j: (j, 0)),
                  pl.BlockSpec((1, tn), lambda i, j: (0, j))],
        out_specs=pl.BlockSpec((tm, tn), lambda i, j: (i, j)),
        out_shape=jax.ShapeDtypeStruct((m, n), out_dtype),
        compiler_params=_params(("parallel", "arbitrary")),
        name=name,
    )(a, wt, scale)


def _rms_bf16(c, g):
    ms = jnp.mean(c * c, axis=-1, keepdims=True)
    return (c * lax.rsqrt(ms + RMS_EPS) * g).astype(BF16)


def _qup_kernel(cq_ref, g_ref, pos_ref, inv_ref, wm_ref, ws_ref, o_ref, cn_ref, cos_ref, sin_ref, *, scale):
    @pl.when(pl.program_id(1) == 0)
    def _():
        cn_ref[...] = _rms_bf16(cq_ref[...], g_ref[...])
        ang = pos_ref[...] * inv_ref[...]
        cos_ref[...] = jnp.cos(ang)
        sin_ref[...] = jnp.sin(ang)

    cn = cn_ref[...]
    main = _dot(cn, wm_ref[...])
    swapped = _dot(cn, ws_ref[...])
    rope = main[:, LANES:] * cos_ref[...] + swapped * sin_ref[...]
    o_ref[:, :LANES] = (main[:, :LANES] * scale).astype(o_ref.dtype)
    o_ref[:, LANES:] = (rope * scale).astype(o_ref.dtype)


def _q_up(lat, g, posb, inv_row, w_main, w_swap, scale, tm):
    m = lat.shape[0]
    nh = MLA_HEADS
    return pl.pallas_call(
        functools.partial(_qup_kernel, scale=scale),
        grid=(m // tm, nh),
        in_specs=[pl.BlockSpec((tm, MLA_Q_RANK), lambda i, h: (i, 0)),
                  pl.BlockSpec((1, MLA_Q_RANK), lambda i, h: (0, 0)),
                  pl.BlockSpec((tm, LANES), lambda i, h: (i, 0)),
                  pl.BlockSpec((1, LANES), lambda i, h: (0, 0)),
                  pl.BlockSpec((MLA_Q_RANK, 2 * LANES), lambda i, h: (0, h)),
                  pl.BlockSpec((MLA_Q_RANK, LANES), lambda i, h: (0, h))],
        out_specs=pl.BlockSpec((tm, 2 * LANES), lambda i, h: (i, h)),
        out_shape=jax.ShapeDtypeStruct((m, nh * 2 * LANES), BF16),
        scratch_shapes=[pltpu.VMEM((tm, MLA_Q_RANK), BF16),
                        pltpu.VMEM((tm, LANES), F32),
                        pltpu.VMEM((tm, LANES), F32)],
        compiler_params=_params(("parallel", "arbitrary")),
        name="mla_q_up",
    )(lat, g, posb, inv_row, w_main, w_swap)


def _kvup_kernel(ckv_ref, kr_ref, krs_ref, g_ref, pos_ref, inv_ref, w_ref, k_ref, v_ref):
    tm = ckv_ref.shape[0]
    cn = _rms_bf16(ckv_ref[...], g_ref[...])
    ang = pos_ref[...] * inv_ref[...]
    krope = (kr_ref[...] * jnp.cos(ang) + krs_ref[...] * jnp.sin(ang)).astype(k_ref.dtype)
    ones_col = (lax.broadcasted_iota(jnp.int32, (tm, LANES), 1) == 0).astype(v_ref.dtype)
    wh = MLA_NOPE_DIM + MLA_V_DIM
    for h in range(MLA_HEADS):
        kv = _dot(cn, w_ref[:, h * wh:(h + 1) * wh])
        k_ref[:, h * wh:h * wh + MLA_NOPE_DIM] = kv[:, :MLA_NOPE_DIM].astype(k_ref.dtype)
        k_ref[:, h * wh + MLA_NOPE_DIM:(h + 1) * wh] = krope
        v_ref[:, h * wh:h * wh + MLA_V_DIM] = kv[:, MLA_NOPE_DIM:].astype(v_ref.dtype)
        v_ref[:, h * wh + MLA_V_DIM:(h + 1) * wh] = ones_col


def _kv_up(lat, g, posb, inv_row, w_ukv, tm):
    m = lat.shape[0]
    n = w_ukv.shape[1]
    ckv_blk = MLA_Q_RANK // MLA_KV_RANK
    kr_blk = (MLA_Q_RANK + MLA_KV_RANK) // LANES
    return pl.pallas_call(
        _kvup_kernel,
        grid=(m // tm,),
        in_specs=[pl.BlockSpec((tm, MLA_KV_RANK), lambda i: (i, ckv_blk)),
                  pl.BlockSpec((tm, LANES), lambda i: (i, kr_blk)),
                  pl.BlockSpec((tm, LANES), lambda i: (i, kr_blk + 1)),
                  pl.BlockSpec((1, MLA_KV_RANK), lambda i: (0, 0)),
                  pl.BlockSpec((tm, LANES), lambda i: (i, 0)),
                  pl.BlockSpec((1, LANES), lambda i: (0, 0)),
                  pl.BlockSpec((MLA_KV_RANK, n), lambda i: (0, 0))],
        out_specs=[pl.BlockSpec((tm, n), lambda i: (i, 0)),
                   pl.BlockSpec((tm, n), lambda i: (i, 0))],
        out_shape=[jax.ShapeDtypeStruct((m, n), BF16),
                   jax.ShapeDtypeStruct((m, n), BF16)],
        compiler_params=_params(("parallel",)),
        name="mla_kv_up",
    )(lat, lat, lat, g, posb, inv_row, w_ukv)


def _row_max(s):
    part = s[:, :LANES]
    for c in range(LANES, s.shape[1], LANES):
        part = jnp.maximum(part, s[:, c:c + LANES])
    return jnp.max(part, axis=1, keepdims=True)


def _lane_partial_sum(p):
    part = p[:, :LANES]
    for c in range(LANES, p.shape[1], LANES):
        part = part + p[:, c:c + LANES]
    return part


def _causal_mask(rows, cols, offset):
    row = lax.broadcasted_iota(jnp.int32, (rows, cols), 0)
    col = lax.broadcasted_iota(jnp.int32, (rows, cols), 1)
    return col <= row + offset


def _da_kernel(slope_ref, q_ref, k_ref, v_ref, pos_ref, lam_ref, g_ref, o_ref,
               m_ref, alpha_ref, l_ref, acc_ref, s_ref, p_ref, *, lambda_init):
    h = pl.program_id(1)
    qi = pl.program_id(2)
    d = DA_HEAD_DIM
    tq = q_ref.shape[0]
    tk = s_ref.shape[2]

    m_ref[...] = jnp.full(m_ref.shape, NEG_BIG, F32)
    l_ref[...] = jnp.zeros(l_ref.shape, F32)
    acc_ref[...] = jnp.zeros(acc_ref.shape, F32)

    slope2 = slope_ref[h] * LOG2E
    q0 = pl.multiple_of(qi * tq, tq)
    pq0 = pos_ref[:, pl.ds(q0, LANES)][:, :1]

    def scores(r, start, width):
        cols = slice(r * d, (r + 1) * d)
        bias = slope2 * (pos_ref[:, pl.ds(start, width)] - pq0)
        return _dot_nt(q_ref[:, cols], k_ref[pl.ds(start, width), cols]) + bias

    def diagonal(start, width, offset):
        v = v_ref[pl.ds(start, width), :]
        for r in range(2):
            s = jnp.where(_causal_mask(tq, width, offset), scores(r, start, width), NEG_BIG)
            m_old = m_ref[r]
            m_new = jnp.maximum(m_old, _row_max(s))
            alpha = jnp.exp2(m_old - m_new)
            p = jnp.exp2(s - m_new)
            l_ref[r] = alpha * l_ref[r] + _lane_partial_sum(p)
            acc_ref[r] = alpha * acc_ref[r] + _dot(p.astype(v.dtype), v)
            m_ref[r] = m_new

    def qk(r, c):
        s_ref[r] = scores(r, pl.multiple_of(c * tk, tk), tk)

    def softmax(r):
        s = s_ref[r]
        m_old = m_ref[r]
        m_new = jnp.maximum(m_old, _row_max(s))
        alpha = jnp.exp2(m_old - m_new)
        p = jnp.exp2(s - m_new)
        alpha_ref[r] = alpha
        l_ref[r] = alpha * l_ref[r] + _lane_partial_sum(p)
        p_ref[r] = p.astype(p_ref.dtype)
        m_ref[r] = m_new

    def pv(r, c):
        v = v_ref[pl.ds(pl.multiple_of(c * tk, tk), tk), :]
        acc_ref[r] = alpha_ref[r] * acc_ref[r] + _dot(p_ref[r], v)

    @pl.when(qi % 2 == 0)
    def _():
        diagonal(q0, tq, 0)

    @pl.when(qi % 2 == 1)
    def _():
        diagonal(pl.multiple_of(q0 - tq, tq), tk, tq)

    _two_stream_pipeline(qi // 2, qk, softmax, pv)

    lam_p = lam_ref[...]
    lam = (jnp.exp(jnp.sum(lam_p[0:1] * lam_p[1:2], axis=1, keepdims=True))
           - jnp.exp(jnp.sum(lam_p[2:3] * lam_p[3:4], axis=1, keepdims=True)) + lambda_init)
    l0 = jnp.sum(l_ref[0], axis=1, keepdims=True)
    l1 = jnp.sum(l_ref[1], axis=1, keepdims=True)
    o = acc_ref[0] / l0 - lam * (acc_ref[1] / l1)
    ms = jnp.mean(o * o, axis=-1, keepdims=True)
    o_ref[...] = (o * lax.rsqrt(ms + RMS_EPS) * g_ref[...] * (1.0 - lambda_init)).astype(o_ref.dtype)


def _diff_attention(qkv, pos_rows, lam_p, g, slopes, batch, seq, tq, lambda_init):
    m = qkv.shape[0]
    nh = DA_HEADS
    nq = seq // tq
    w = DA_V_DIM
    grid_spec = pltpu.PrefetchScalarGridSpec(
        num_scalar_prefetch=1,
        grid=(batch, nh, nq),
        in_specs=[pl.BlockSpec((tq, w), lambda b, h, qi, sl: (b * nq + qi, h)),
                  pl.BlockSpec((seq, w), lambda b, h, qi, sl: (b, nh + h)),
                  pl.BlockSpec((seq, w), lambda b, h, qi, sl: (b, 2 * nh + h)),
                  pl.BlockSpec((None, 1, seq), lambda b, h, qi, sl: (b, 0, 0)),
                  pl.BlockSpec((4, DA_HEAD_DIM), lambda b, h, qi, sl: (0, 0)),
                  pl.BlockSpec((1, w), lambda b, h, qi, sl: (0, 0))],
        out_specs=pl.BlockSpec((tq, w), lambda b, h, qi, sl: (b * nq + qi, h)),
        scratch_shapes=[pltpu.VMEM((2, tq, 1), F32),
                        pltpu.VMEM((2, tq, 1), F32),
                        pltpu.VMEM((2, tq, LANES), F32),
                        pltpu.VMEM((2, tq, w), F32),
                        pltpu.VMEM((2, tq, 2 * tq), F32),
                        pltpu.VMEM((2, tq, 2 * tq), BF16)],
    )
    return pl.pallas_call(
        functools.partial(_da_kernel, lambda_init=lambda_init),
        grid_spec=grid_spec,
        out_shape=jax.ShapeDtypeStruct((m, nh * w), BF16),
        compiler_params=_params(("parallel", "parallel", "arbitrary")),
        name="diff_attention",
    )(slopes, qkv, qkv, qkv, pos_rows, lam_p, g)


def _two_stream_pipeline(n, qk, softmax, pv):
    @pl.when(n > 0)
    def _():
        qk(0, 0)
        softmax(0)
        qk(1, 0)

        def trip(c, carry):
            pv(0, c - 1)
            softmax(1)
            qk(0, c)
            pv(1, c - 1)
            softmax(0)
            qk(1, c)
            return carry

        lax.fori_loop(1, n, trip, 0)
        pv(0, n - 1)
        softmax(1)
        pv(1, n - 1)


def _mla_kernel(q_ref, k_ref, v_ref, o_ref, m_ref, alpha_ref, acc_ref, s_ref, p_ref):
    qi = pl.program_id(2)
    tq = q_ref.shape[0]
    half = tq // 2

    m_ref[...] = jnp.full(m_ref.shape, NEG_BIG, F32)
    acc_ref[...] = jnp.zeros(acc_ref.shape, F32)

    def rows(r):
        return slice(r * half, (r + 1) * half)

    def diagonal(r, start, width, offset):
        s = _dot_nt(q_ref[rows(r), :], k_ref[pl.ds(start, width), :])
        s = jnp.where(_causal_mask(half, width, offset), s, NEG_BIG)
        m_old = m_ref[r]
        m_new = jnp.maximum(m_old, _row_max(s))
        p = jnp.exp2(s - m_new).astype(v_ref.dtype)
        acc_ref[r] = jnp.exp2(m_old - m_new) * acc_ref[r] + _dot(p, v_ref[pl.ds(start, width), :])
        m_ref[r] = m_new

    def qk(r, c):
        s_ref[r] = _dot_nt(q_ref[rows(r), :], k_ref[pl.ds(pl.multiple_of(c * tq, tq), tq), :])

    def softmax(r):
        s = s_ref[r]
        m_old = m_ref[r]
        m_new = jnp.maximum(m_old, _row_max(s))
        alpha_ref[r] = jnp.exp2(m_old - m_new)
        p_ref[r] = jnp.exp2(s - m_new).astype(p_ref.dtype)
        m_ref[r] = m_new

    def pv(r, c):
        v = v_ref[pl.ds(pl.multiple_of(c * tq, tq), tq), :]
        acc_ref[r] = alpha_ref[r] * acc_ref[r] + _dot(p_ref[r], v)

    d0 = pl.multiple_of(qi * tq, tq)
    diagonal(0, d0, half, 0)
    diagonal(1, d0, tq, half)
    _two_stream_pipeline(qi, qk, softmax, pv)
    for r in range(2):
        acc = acc_ref[r]
        o_ref[rows(r), :] = (acc[:, :MLA_V_DIM] / acc[:, MLA_V_DIM:MLA_V_DIM + 1]).astype(o_ref.dtype)


def _mla_attention(qcat, kcat, vext, batch, seq, tq):
    m = qcat.shape[0]
    nh = MLA_HEADS
    nq = seq // tq
    w = 2 * LANES
    return pl.pallas_call(
        _mla_kernel,
        grid=(batch, nh, nq),
        in_specs=[pl.BlockSpec((tq, w), lambda b, h, qi: (b * nq + qi, h)),
                  pl.BlockSpec((seq, w), lambda b, h, qi: (b, h)),
                  pl.BlockSpec((seq, w), lambda b, h, qi: (b, h))],
        out_specs=pl.BlockSpec((tq, MLA_V_DIM), lambda b, h, qi: (b * nq + qi, h)),
        out_shape=jax.ShapeDtypeStruct((m, nh * MLA_V_DIM), BF16),
        scratch_shapes=[pltpu.VMEM((2, tq // 2, 1), F32),
                        pltpu.VMEM((2, tq // 2, 1), F32),
                        pltpu.VMEM((2, tq // 2, w), F32),
                        pltpu.VMEM((2, tq // 2, tq), F32),
                        pltpu.VMEM((2, tq // 2, tq), BF16)],
        compiler_params=_params(("parallel", "parallel", "arbitrary")),
        name="mla_attention",
    )(qcat, kcat, vext)


def _merge_kernel(x_ref, oa_ref, ob_ref, wga_ref, wgb_ref, wpa_ref, wpb_ref, ba_ref, bb_ref, o_ref):
    d = x_ref.shape[1]
    kc = oa_ref.shape[1] if d % oa_ref.shape[1] == 0 else d

    def gate(wt_ref, b_ref):
        acc = b_ref[...] + _dot_nt(x_ref[:, :kc], wt_ref[:, :kc])
        for c in range(kc, d, kc):
            acc = acc + _dot_nt(x_ref[:, c:c + kc], wt_ref[:, c:c + kc])
        return jax.nn.sigmoid(acc)

    ga = gate(wga_ref, ba_ref)
    gb = gate(wgb_ref, bb_ref)
    pa = _dot(oa_ref[...], wpa_ref[...])
    pb = _dot(ob_ref[...], wpb_ref[...])
    o_ref[...] = (ga * pa + gb * pb).astype(o_ref.dtype)


def _gated_merge(xb, oa, ob, wga_t, wgb_t, wpa, wpb, ba, bb, tm, tn):
    m, d = xb.shape
    n = wpa.shape[1]
    ko = oa.shape[1]
    return pl.pallas_call(
        _merge_kernel,
        grid=(m // tm, n // tn),
        in_specs=[pl.BlockSpec((tm, d), lambda i, j: (i, 0)),
                  pl.BlockSpec((tm, ko), lambda i, j: (i, 0)),
                  pl.BlockSpec((tm, ko), lambda i, j: (i, 0)),
                  pl.BlockSpec((tn, d), lambda i, j: (j, 0)),
                  pl.BlockSpec((tn, d), lambda i, j: (j, 0)),
                  pl.BlockSpec((ko, tn), lambda i, j: (0, j)),
                  pl.BlockSpec((ko, tn), lambda i, j: (0, j)),
                  pl.BlockSpec((1, tn), lambda i, j: (0, j)),
                  pl.BlockSpec((1, tn), lambda i, j: (0, j))],
        out_specs=pl.BlockSpec((tm, tn), lambda i, j: (i, j)),
        out_shape=jax.ShapeDtypeStruct((m, n), BF16),
        compiler_params=_params(("parallel", "arbitrary")),
        name="gated_merge",
    )(xb, oa, ob, wga_t, wgb_t, wpa, wpb, ba, bb)


def _mm_res_kernel(a_ref, w_ref, r_ref, o_ref, *, alpha):
    o_ref[...] = alpha * r_ref[...] + _dot(a_ref[...], w_ref[...])


def _mm_res(a, w, res, alpha, tm, tn):
    m, k = a.shape
    n = w.shape[1]
    return pl.pallas_call(
        functools.partial(_mm_res_kernel, alpha=alpha),
        grid=(m // tm, n // tn),
        in_specs=[pl.BlockSpec((tm, k), lambda i, j: (i, 0)),
                  pl.BlockSpec((k, tn), lambda i, j: (0, j)),
                  pl.BlockSpec((tm, tn), lambda i, j: (i, j))],
        out_specs=pl.BlockSpec((tm, tn), lambda i, j: (i, j)),
        out_shape=jax.ShapeDtypeStruct((m, n), F32),
        compiler_params=_params(("parallel", "arbitrary")),
        name="out_proj_residual",
    )(a, w, res)


def _layernorm(y, g, b):
    mu = jnp.mean(y, axis=-1, keepdims=True)
    yc = y - mu
    var = jnp.mean(yc * yc, axis=-1, keepdims=True)
    return yc * lax.rsqrt(var + LN_EPS) * g + b


def _ffn_up_kernel(y_ref, g_ref, b_ref, w_ref, cw_ref, cb_ref, o_ref, h_ref, u_ref, carry_ref, *, nj, tiles_per_seq):
    t = pl.program_id(0)
    last = pl.num_programs(0) - 1
    tm = y_ref.shape[0]
    half = o_ref.shape[1]
    pad = SUBLANES

    @pl.when(t == 0)
    def _():
        u_ref[...] = jnp.zeros(u_ref.shape, F32)
        carry_ref[...] = jnp.zeros(carry_ref.shape, F32)

    @pl.when((t % nj == 0) & (t < last))
    def _():
        h_ref[...] = _layernorm(y_ref[...], g_ref[...], b_ref[...]).astype(h_ref.dtype)

    tp = jnp.maximum(t - 1, 0)
    ip = tp // nj
    jp = tp % nj
    seq_start = (ip % tiles_per_seq == 0).astype(F32)
    u_ref[:pad, :] = carry_ref[jp] * (1.0 - seq_start)
    cw = cw_ref[...]
    c = cb_ref[...] + cw[2:3] * u_ref[pad:, :]
    for back in range(1, CONV_WIDTH):
        c = c + cw[CONV_WIDTH - 1 - back:CONV_WIDTH - back] * u_ref[pad - back:pad - back + tm, :]
    carry_ref[jp] = u_ref[tm:, :]
    gate = c[:, :half]
    o_ref[...] = (gate * jax.nn.sigmoid(gate) * c[:, half:]).astype(o_ref.dtype)

    u_ref[pad:, :] = _dot(h_ref[...], w_ref[...])


def _ffn_up(y, g, b, w_up, conv_w, conv_b, seq, tm, tn):
    m, d = y.shape
    n2 = w_up.shape[1]
    nj = n2 // (2 * tn)
    steps = (m // tm) * nj

    def cur(t):
        return jnp.minimum(t, steps - 1)

    def prev(t):
        return jnp.maximum(t - 1, 0)

    return pl.pallas_call(
        functools.partial(_ffn_up_kernel, nj=nj, tiles_per_seq=seq // tm),
        grid=(steps + 1,),
        in_specs=[pl.BlockSpec((tm, d), lambda t: (cur(t) // nj, 0)),
                  pl.BlockSpec((1, d), lambda t: (0, 0)),
                  pl.BlockSpec((1, d), lambda t: (0, 0)),
                  pl.BlockSpec((d, 2 * tn), lambda t: (0, cur(t) % nj)),
                  pl.BlockSpec((CONV_WIDTH, 2 * tn), lambda t: (0, prev(t) % nj)),
                  pl.BlockSpec((1, 2 * tn), lambda t: (0, prev(t) % nj))],
        out_specs=pl.BlockSpec((tm, tn), lambda t: (prev(t) // nj, prev(t) % nj)),
        out_shape=jax.ShapeDtypeStruct((m, n2 // 2), BF16),
        scratch_shapes=[pltpu.VMEM((tm, d), BF16),
                        pltpu.VMEM((tm + SUBLANES, 2 * tn), F32),
                        pltpu.VMEM((nj, SUBLANES, 2 * tn), F32)],
        compiler_params=_params(("arbitrary",)),
        name="ffn_up_conv_gate",
    )(y, g, b, w_up, conv_w, conv_b)


def _ffn_down_kernel(a_ref, w_ref, y_ref, g1_ref, b1_ref, g2_ref, b2_ref, o_ref, *, alpha):
    k = pl.program_id(1)
    tm, d = o_ref.shape

    @pl.when(k == 0)
    def _():
        o_ref[...] = jnp.zeros(o_ref.shape, F32)

    a = a_ref[...]
    for c in range(0, d, FFN_DOWN_COL_CHUNK):
        cols = slice(c, min(c + FFN_DOWN_COL_CHUNK, d))
        o_ref[:, cols] += _dot(a, w_ref[:, cols])

    @pl.when(k == pl.num_programs(1) - 1)
    def _():
        rows_per = min(tm, FFN_DOWN_ROW_CHUNK)

        def body(r, carry):
            rows = pl.ds(pl.multiple_of(r * rows_per, rows_per), rows_per)
            h = _layernorm(y_ref[rows, :], g1_ref[...], b1_ref[...])
            o_ref[rows, :] = _layernorm(alpha * h + o_ref[rows, :], g2_ref[...], b2_ref[...])
            return carry

        lax.fori_loop(0, tm // rows_per, body, 0)


def _ffn_down(act, w_down, y, g1, b1, g2, b2, alpha, tm, tk):
    m, kdim = act.shape
    d = w_down.shape[1]
    vec = pl.BlockSpec((1, d), lambda i, k: (0, 0))
    return pl.pallas_call(
        functools.partial(_ffn_down_kernel, alpha=alpha),
        grid=(m // tm, kdim // tk),
        in_specs=[pl.BlockSpec((tm, tk), lambda i, k: (i, k)),
                  pl.BlockSpec((tk, d), lambda i, k: (k, 0)),
                  pl.BlockSpec((tm, d), lambda i, k: (i, 0)),
                  vec, vec, vec, vec],
        out_specs=pl.BlockSpec((tm, d), lambda i, k: (i, 0)),
        out_shape=jax.ShapeDtypeStruct((m, d), F32),
        compiler_params=_params(("parallel", "arbitrary")),
        name="ffn_down_layernorm",
    )(act, w_down, y, g1, b1, g2, b2)


def _cast_kernel(x_ref, o_ref):
    o_ref[...] = x_ref[...].astype(o_ref.dtype)


def _cast_bf16(w, rb, cb, name):
    rows, cols = w.shape
    return pl.pallas_call(
        _cast_kernel,
        grid=(pl.cdiv(rows, rb), pl.cdiv(cols, cb)),
        in_specs=[pl.BlockSpec((rb, cb), lambda i, j: (i, j))],
        out_specs=pl.BlockSpec((rb, cb), lambda i, j: (i, j)),
        out_shape=jax.ShapeDtypeStruct((rows, cols), BF16),
        compiler_params=_params(("parallel", "parallel")),
        name=name,
    )(w)


def _regroup_kernel(src_ref, keep_ref, x_ref, o_ref):
    keep = keep_ref[pl.program_id(0)] > 0
    o_ref[...] = jnp.where(keep, x_ref[...], 0.0).astype(o_ref.dtype)


def _regroup_cast_w_up(w_up, f, tn, nj):
    d = w_up.shape[0]
    cb = math.gcd(f, tn)
    sub = tn // cb
    nsrc = f // cb
    src, keep = [], []
    for j in range(nj):
        for base in (0, nsrc):
            for r in range(sub):
                g = j * sub + r
                src.append(base + min(g, nsrc - 1))
                keep.append(1 if g < nsrc else 0)
    grid_spec = pltpu.PrefetchScalarGridSpec(
        num_scalar_prefetch=2,
        grid=(len(src),),
        in_specs=[pl.BlockSpec((d, cb), lambda b, src, keep: (0, src[b]))],
        out_specs=pl.BlockSpec((d, cb), lambda b, src, keep: (0, b)),
    )
    return pl.pallas_call(
        _regroup_kernel,
        grid_spec=grid_spec,
        out_shape=jax.ShapeDtypeStruct((d, nj * 2 * tn), BF16),
        compiler_params=_params(("parallel",)),
        name="w_up_regroup_cast",
    )(jnp.asarray(src, jnp.int32), jnp.asarray(keep, jnp.int32), w_up)


def _rope_rows(wt):
    half = MLA_ROPE_DIM // 2
    return jnp.concatenate([-wt[half:], wt[:half]], axis=0)


def _pad_rows(w, n):
    return jnp.pad(w, ((0, n - w.shape[0]), (0, 0)))


def _pad_cols(w, n):
    return jnp.pad(w, ((0, 0), (0, n - w.shape[1])))


def _layer(h, posb, pos_rows, batch, seq, alpha, lambda_init, w_in, b_gate, lam_p, da_g, q_g, kv_g, w_uq, w_ukv,
           w_proj_a, w_proj_b, w_out, ln1_g, ln1_b, w_up, conv_w, conv_b, w_down, ln2_g, ln2_b):
    m, d = h.shape
    f = w_down.shape[0]
    da_cols = 3 * DA_HEADS * DA_V_DIM
    off_cq = da_cols
    off_ckv = off_cq + MLA_Q_RANK
    off_kr = off_ckv + MLA_KV_RANK
    off_gate = off_kr + MLA_ROPE_DIM

    hb = h.astype(BF16)

    da_scale = DA_HEAD_DIM ** -0.5 * LOG2E
    qkv_scale = jnp.concatenate([jnp.full((DA_HEADS * DA_V_DIM,), da_scale, F32),
                                 jnp.ones((2 * DA_HEADS * DA_V_DIM,), F32)])[None, :]
    wt_in = _cast_bf16(jnp.swapaxes(w_in, 0, 1), 1024, _tile(d, 1024), "w_in_cast")
    qkv = _mm_scale(hb, wt_in, qkv_scale, BF16, _tile(m, 1024), 512, "da_qkv_proj")

    wt_kr = wt_in[off_kr:off_gate]
    wt_lat = jnp.concatenate([wt_in[off_cq:off_kr], _pad_rows(wt_kr, LANES), _pad_rows(_rope_rows(wt_kr), LANES)], axis=0)
    lat = _mm_scale(hb, wt_lat, jnp.ones((1, wt_lat.shape[0]), F32), F32, _tile(m, 1024), wt_lat.shape[0] // 2,
                    "mla_latent_proj")

    slopes = 2.0 ** (-8.0 * jnp.arange(1, DA_HEADS + 1, dtype=F32) / DA_HEADS)
    oa = _diff_attention(qkv, pos_rows, lam_p, da_g[None, :], slopes, batch, seq, _tile(seq, 512), lambda_init)

    half = MLA_ROPE_DIM // 2
    inv = ROPE_THETA ** (-jnp.arange(half, dtype=F32) / half)
    inv_row = _pad_cols(jnp.concatenate([inv, inv])[None, :], LANES)
    hd = MLA_NOPE_DIM + MLA_ROPE_DIM
    wq = w_uq.reshape(MLA_Q_RANK, MLA_HEADS, hd)
    zeros = jnp.zeros((MLA_Q_RANK, MLA_HEADS, LANES - MLA_ROPE_DIM), F32)
    wq_main = jnp.concatenate([wq, zeros], axis=2).reshape(MLA_Q_RANK, MLA_HEADS * 2 * LANES).astype(BF16)
    wq_swap = jnp.concatenate([-wq[:, :, MLA_NOPE_DIM + half:], wq[:, :, MLA_NOPE_DIM:MLA_NOPE_DIM + half], zeros],
                              axis=2).reshape(MLA_Q_RANK, MLA_HEADS * LANES).astype(BF16)
    qcat = _q_up(lat, q_g[None, :], posb, inv_row, wq_main, wq_swap, hd ** -0.5 * LOG2E, _tile(m, 1024))
    kcat, vext = _kv_up(lat, kv_g[None, :], posb, inv_row, w_ukv.astype(BF16), _tile(m, 512))
    ob = _mla_attention(qcat, kcat, vext, batch, seq, _tile(seq, 1024))

    merged = _gated_merge(hb, oa, ob, wt_in[off_gate:off_gate + d], wt_in[off_gate + d:],
                          w_proj_a.astype(BF16), w_proj_b.astype(BF16), b_gate[None, :d], b_gate[None, d:],
                          _tile(m, 512), _tile(d, 512))
    y1 = _mm_res(merged, w_out.astype(BF16), h, alpha, _tile(m, 1024), _tile(d, 512))

    tn = 512
    fp = -(-f // tn) * tn
    nj = fp // tn

    def group(a):
        r = a.shape[0]
        gv = jnp.stack([_pad_cols(a[:, :f], fp).reshape(r, nj, tn), _pad_cols(a[:, f:], fp).reshape(r, nj, tn)], axis=2)
        return gv.reshape(r, nj * 2 * tn)

    act = _ffn_up(y1, ln1_g[None, :], ln1_b[None, :], _regroup_cast_w_up(w_up, f, tn, nj), group(conv_w),
                  group(conv_b[None, :]), seq, _tile(seq, 512), tn)
    w_down_p = jnp.pad(w_down.astype(BF16), ((0, fp - f), (0, 0)))
    return _ffn_down(act, w_down_p, y1, ln1_g[None, :], ln1_b[None, :], ln2_g[None, :], ln2_b[None, :], alpha,
                     _tile(m, 512), 512)


def kernel(x, positions, w_in, b_gate, da_lambda_q1, da_lambda_k1, da_lambda_q2, da_lambda_k2, da_subln_g, mla_q_norm_g, mla_kv_norm_g, w_uq, w_ukv, w_proj_a, w_proj_b, w_out, ln1_g, ln1_b, w_up, conv_w, conv_b, w_down, ln2_g, ln2_b):
    batch, seq, d = x.shape
    depth = w_in.shape[0]
    alpha = (2.0 * depth) ** 0.25
    m = batch * seq
    posf = positions.astype(F32).reshape(m)
    posb = jnp.broadcast_to(posf[:, None], (m, LANES))
    pos_rows = posf.reshape(batch, 1, seq)
    h = x.reshape(m, d)
    for l in range(depth):
        lambda_init = 0.8 - 0.6 * math.exp(-0.3 * l)
        lam_p = jnp.stack([da_lambda_q1[l], da_lambda_k1[l], da_lambda_q2[l], da_lambda_k2[l]]).astype(F32)
        h = _layer(h, posb, pos_rows, batch, seq, alpha, lambda_init, w_in[l], b_gate[l], lam_p, da_subln_g[l],
                   mla_q_norm_g[l], mla_kv_norm_g[l], w_uq[l], w_ukv[l], w_proj_a[l], w_proj_b[l], w_out[l],
                   ln1_g[l], ln1_b[l], w_up[l], conv_w[l], conv_b[l], w_down[l], ln2_g[l], ln2_b[l])
    return h.reshape(batch, seq, d)
```

```python
import functools
import math

import numpy as np
import jax
import jax.numpy as jnp
from jax import lax
from jax.experimental import pallas as pl
from jax.experimental.pallas import tpu as pltpu

DA_HEADS = 8
DA_HEAD_DIM = 128
DA_V_DIM = 2 * DA_HEAD_DIM
MLA_HEADS = 16
MLA_Q_RANK = 1024
MLA_KV_RANK = 512
MLA_NOPE_DIM = 128
MLA_ROPE_DIM = 64
MLA_V_DIM = 128
ROPE_THETA = 10000.0
CONV_WIDTH = 3
LN_EPS = 1e-5
RMS_EPS = 1e-6

LANES = 128
SUBLANES = 8
VMEM_LIMIT = 56 * 1024 * 1024
NEG_BIG = -1e30
LOG2E = math.log2(math.e)
Q_UP_HEADS_PER_STEP = 4
FFN_DOWN_COL_CHUNK = 1024
FFN_DOWN_ROW_CHUNK = 64

F32 = jnp.float32
BF16 = jnp.bfloat16


def _params(sem):
    return pltpu.CompilerParams(dimension_semantics=sem, vmem_limit_bytes=VMEM_LIMIT)


def _tile(n, pref):
    t = min(n, pref)
    while n % t:
        t //= 2
    return t


def _dot(a, b):
    return jnp.dot(a, b, preferred_element_type=F32)


def _dot_nt(a, b):
    return lax.dot_general(a, b, (((1,), (1,)), ((), ())), preferred_element_type=F32)


def _mm_scale_kernel(a_ref, wt_ref, s_ref, o_ref):
    o_ref[...] = (_dot_nt(a_ref[...], wt_ref[...]) * s_ref[...]).astype(o_ref.dtype)


def _mm_scale(a, wt, scale, out_dtype, tm, tn, name):
    m, k = a.shape
    n = scale.shape[1]
    return pl.pallas_call(
        _mm_scale_kernel,
        grid=(m // tm, n // tn),
        in_specs=[pl.BlockSpec((tm, k), lambda i, j: (i, 0)),
                  pl.BlockSpec((tn, k), lambda i, j: (j, 0)),
                  pl.BlockSpec((1, tn), lambda i, j: (0, j))],
        out_specs=pl.BlockSpec((tm, tn), lambda i, j: (i, j)),
        out_shape=jax.ShapeDtypeStruct((m, n), out_dtype),
        compiler_params=_params(("parallel", "arbitrary")),
        name=name,
    )(a, wt, scale)


def _rms_bf16(c, g):
    ms = jnp.mean(c * c, axis=-1, keepdims=True)
    return (c * lax.rsqrt(ms + RMS_EPS) * g).astype(BF16)


def _qup_kernel(cq_ref, g_ref, pos_ref, inv_ref, wm_ref, ws_ref, o_ref, cn_ref, cos_ref, sin_ref, *, scale):
    @pl.when(pl.program_id(1) == 0)
    def _():
        cn_ref[...] = _rms_bf16(cq_ref[...], g_ref[...])
        ang = pos_ref[...] * inv_ref[...]
        cos_ref[...] = jnp.cos(ang)
        sin_ref[...] = jnp.sin(ang)

    cn = cn_ref[...]
    main = _dot(cn, wm_ref[...])
    swapped = _dot(cn, ws_ref[...])
    for hh in range(o_ref.shape[1] // (2 * LANES)):
        c0 = hh * 2 * LANES
        rope = main[:, c0 + LANES:c0 + 2 * LANES] * cos_ref[...] + swapped[:, hh * LANES:(hh + 1) * LANES] * sin_ref[...]
        o_ref[:, c0:c0 + LANES] = (main[:, c0:c0 + LANES] * scale).astype(o_ref.dtype)
        o_ref[:, c0 + LANES:c0 + 2 * LANES] = (rope * scale).astype(o_ref.dtype)


def _q_up(lat, g, posb, inv_row, w_main, w_swap, scale, tm):
    m = lat.shape[0]
    nh = MLA_HEADS
    hg = Q_UP_HEADS_PER_STEP
    return pl.pallas_call(
        functools.partial(_qup_kernel, scale=scale),
        grid=(m // tm, nh // hg),
        in_specs=[pl.BlockSpec((tm, MLA_Q_RANK), lambda i, h: (i, 0)),
                  pl.BlockSpec((1, MLA_Q_RANK), lambda i, h: (0, 0)),
                  pl.BlockSpec((tm, LANES), lambda i, h: (i, 0)),
                  pl.BlockSpec((1, LANES), lambda i, h: (0, 0)),
                  pl.BlockSpec((MLA_Q_RANK, hg * 2 * LANES), lambda i, h: (0, h)),
                  pl.BlockSpec((MLA_Q_RANK, hg * LANES), lambda i, h: (0, h))],
        out_specs=pl.BlockSpec((tm, hg * 2 * LANES), lambda i, h: (i, h)),
        out_shape=jax.ShapeDtypeStruct((m, nh * 2 * LANES), BF16),
        scratch_shapes=[pltpu.VMEM((tm, MLA_Q_RANK), BF16),
                        pltpu.VMEM((tm, LANES), F32),
                        pltpu.VMEM((tm, LANES), F32)],
        compiler_params=_params(("parallel", "arbitrary")),
        name="mla_q_up",
    )(lat, g, posb, inv_row, w_main, w_swap)


def _kvup_kernel(ckv_ref, kr_ref, krs_ref, g_ref, pos_ref, inv_ref, w_ref, k_ref, v_ref):
    tm = ckv_ref.shape[0]
    cn = _rms_bf16(ckv_ref[...], g_ref[...])
    ang = pos_ref[...] * inv_ref[...]
    krope = (kr_ref[...] * jnp.cos(ang) + krs_ref[...] * jnp.sin(ang)).astype(k_ref.dtype)
    ones_col = (lax.broadcasted_iota(jnp.int32, (tm, LANES), 1) == 0).astype(v_ref.dtype)
    wh = MLA_NOPE_DIM + MLA_V_DIM
    for h in range(MLA_HEADS):
        kv = _dot(cn, w_ref[:, h * wh:(h + 1) * wh])
        k_ref[:, h * wh:h * wh + MLA_NOPE_DIM] = kv[:, :MLA_NOPE_DIM].astype(k_ref.dtype)
        k_ref[:, h * wh + MLA_NOPE_DIM:(h + 1) * wh] = krope
        v_ref[:, h * wh:h * wh + MLA_V_DIM] = kv[:, MLA_NOPE_DIM:].astype(v_ref.dtype)
        v_ref[:, h * wh + MLA_V_DIM:(h + 1) * wh] = ones_col


def _kv_up(lat, g, posb, inv_row, w_ukv, tm):
    m = lat.shape[0]
    n = w_ukv.shape[1]
    ckv_blk = MLA_Q_RANK // MLA_KV_RANK
    kr_blk = (MLA_Q_RANK + MLA_KV_RANK) // LANES
    return pl.pallas_call(
        _kvup_kernel,
        grid=(m // tm,),
        in_specs=[pl.BlockSpec((tm, MLA_KV_RANK), lambda i: (i, ckv_blk)),
                  pl.BlockSpec((tm, LANES), lambda i: (i, kr_blk)),
                  pl.BlockSpec((tm, LANES), lambda i: (i, kr_blk + 1)),
                  pl.BlockSpec((1, MLA_KV_RANK), lambda i: (0, 0)),
                  pl.BlockSpec((tm, LANES), lambda i: (i, 0)),
                  pl.BlockSpec((1, LANES), lambda i: (0, 0)),
                  pl.BlockSpec((MLA_KV_RANK, n), lambda i: (0, 0))],
        out_specs=[pl.BlockSpec((tm, n), lambda i: (i, 0)),
                   pl.BlockSpec((tm, n), lambda i: (i, 0))],
        out_shape=[jax.ShapeDtypeStruct((m, n), BF16),
                   jax.ShapeDtypeStruct((m, n), BF16)],
        compiler_params=_params(("parallel",)),
        name="mla_kv_up",
    )(lat, lat, lat, g, posb, inv_row, w_ukv)


def _row_max(s):
    part = s[:, :LANES]
    for c in range(LANES, s.shape[1], LANES):
        part = jnp.maximum(part, s[:, c:c + LANES])
    return jnp.max(part, axis=1, keepdims=True)


def _lane_partial_sum(p):
    part = p[:, :LANES]
    for c in range(LANES, p.shape[1], LANES):
        part = part + p[:, c:c + LANES]
    return part


def _causal_mask(rows, cols, offset):
    row = lax.broadcasted_iota(jnp.int32, (rows, cols), 0)
    col = lax.broadcasted_iota(jnp.int32, (rows, cols), 1)
    return col <= row + offset


def _da_kernel(slope_ref, q_ref, k_ref, v_ref, pos_ref, lam_ref, g_ref, o_ref,
               m_ref, alpha_ref, l_ref, acc_ref, s_ref, p_ref, *, lambda_init):
    h = pl.program_id(1)
    qi = pl.program_id(2)
    d = DA_HEAD_DIM
    tq = q_ref.shape[0]
    tk = s_ref.shape[2]

    m_ref[...] = jnp.full(m_ref.shape, NEG_BIG, F32)
    l_ref[...] = jnp.zeros(l_ref.shape, F32)
    acc_ref[...] = jnp.zeros(acc_ref.shape, F32)

    slope2 = slope_ref[h] * LOG2E
    q0 = pl.multiple_of(qi * tq, tq)
    pq0 = pos_ref[:, pl.ds(q0, LANES)][:, :1]

    def scores(r, start, width):
        cols = slice(r * d, (r + 1) * d)
        bias = slope2 * (pos_ref[:, pl.ds(start, width)] - pq0)
        return _dot_nt(q_ref[:, cols], k_ref[pl.ds(start, width), cols]) + bias

    def diagonal(start, width, offset):
        v = v_ref[pl.ds(start, width), :]
        for r in range(2):
            s = jnp.where(_causal_mask(tq, width, offset), scores(r, start, width), NEG_BIG)
            m_old = m_ref[r]
            m_new = jnp.maximum(m_old, _row_max(s))
            alpha = jnp.exp2(m_old - m_new)
            p = jnp.exp2(s - m_new)
            l_ref[r] = alpha * l_ref[r] + _lane_partial_sum(p)
            acc_ref[r] = alpha * acc_ref[r] + _dot(p.astype(v.dtype), v)
            m_ref[r] = m_new

    def qk(r, c):
        s_ref[r] = scores(r, pl.multiple_of(c * tk, tk), tk)

    def softmax(r):
        s = s_ref[r]
        m_old = m_ref[r]
        m_new = jnp.maximum(m_old, _row_max(s))
        alpha = jnp.exp2(m_old - m_new)
        p = jnp.exp2(s - m_new)
        alpha_ref[r] = alpha
        l_ref[r] = alpha * l_ref[r] + _lane_partial_sum(p)
        p_ref[r] = p.astype(p_ref.dtype)
        m_ref[r] = m_new

    def pv(r, c):
        v = v_ref[pl.ds(pl.multiple_of(c * tk, tk), tk), :]
        acc_ref[r] = alpha_ref[r] * acc_ref[r] + _dot(p_ref[r], v)

    @pl.when(qi % 2 == 0)
    def _():
        _two_stream_pipeline(qi // 2, lambda: diagonal(q0, tq, 0), qk, softmax, pv)

    @pl.when(qi % 2 == 1)
    def _():
        _two_stream_pipeline(qi // 2, lambda: diagonal(pl.multiple_of(q0 - tq, tq), tk, tq), qk, softmax, pv)

    lam_p = lam_ref[...]
    lam = (jnp.exp(jnp.sum(lam_p[0:1] * lam_p[1:2], axis=1, keepdims=True))
           - jnp.exp(jnp.sum(lam_p[2:3] * lam_p[3:4], axis=1, keepdims=True)) + lambda_init)
    l0 = jnp.sum(l_ref[0], axis=1, keepdims=True)
    l1 = jnp.sum(l_ref[1], axis=1, keepdims=True)
    o = acc_ref[0] / l0 - lam * (acc_ref[1] / l1)
    ms = jnp.mean(o * o, axis=-1, keepdims=True)
    o_ref[...] = (o * lax.rsqrt(ms + RMS_EPS) * g_ref[...] * (1.0 - lambda_init)).astype(o_ref.dtype)


def _diff_attention(qkv, pos_rows, lam_p, g, slopes, batch, seq, tq, lambda_init):
    m = qkv.shape[0]
    nh = DA_HEADS
    nq = seq // tq
    w = DA_V_DIM
    grid_spec = pltpu.PrefetchScalarGridSpec(
        num_scalar_prefetch=1,
        grid=(batch, nh, nq),
        in_specs=[pl.BlockSpec((tq, w), lambda b, h, qi, sl: (b * nq + qi, h)),
                  pl.BlockSpec((seq, w), lambda b, h, qi, sl: (b, nh + h)),
                  pl.BlockSpec((seq, w), lambda b, h, qi, sl: (b, 2 * nh + h)),
                  pl.BlockSpec((None, 1, seq), lambda b, h, qi, sl: (b, 0, 0)),
                  pl.BlockSpec((4, DA_HEAD_DIM), lambda b, h, qi, sl: (0, 0)),
                  pl.BlockSpec((1, w), lambda b, h, qi, sl: (0, 0))],
        out_specs=pl.BlockSpec((tq, w), lambda b, h, qi, sl: (b * nq + qi, h)),
        scratch_shapes=[pltpu.VMEM((2, tq, 1), F32),
                        pltpu.VMEM((2, tq, 1), F32),
                        pltpu.VMEM((2, tq, LANES), F32),
                        pltpu.VMEM((2, tq, w), F32),
                        pltpu.VMEM((2, tq, 2 * tq), F32),
                        pltpu.VMEM((2, tq, 2 * tq), BF16)],
    )
    return pl.pallas_call(
        functools.partial(_da_kernel, lambda_init=lambda_init),
        grid_spec=grid_spec,
        out_shape=jax.ShapeDtypeStruct((m, nh * w), BF16),
        compiler_params=_params(("parallel", "parallel", "arbitrary")),
        name="diff_attention",
    )(slopes, qkv, qkv, qkv, pos_rows, lam_p, g)


def _two_stream_pipeline(n, head, qk, softmax, pv):
    @pl.when(n == 0)
    def _():
        head()

    @pl.when(n > 0)
    def _():
        head()
        qk(0, 0)
        softmax(0)
        qk(1, 0)

        def trip(c, carry):
            pv(0, c - 1)
            softmax(1)
            qk(0, c)
            pv(1, c - 1)
            softmax(0)
            qk(1, c)
            return carry

        lax.fori_loop(1, n, trip, 0)
        pv(0, n - 1)
        softmax(1)
        pv(1, n - 1)


def _mla_kernel(q_ref, k_ref, v_ref, o_ref, m_ref, alpha_ref, acc_ref, s_ref, p_ref):
    qi = pl.program_id(2)
    tq = q_ref.shape[0]
    half = tq // 2

    m_ref[...] = jnp.full(m_ref.shape, NEG_BIG, F32)
    acc_ref[...] = jnp.zeros(acc_ref.shape, F32)

    def rows(r):
        return slice(r * half, (r + 1) * half)

    def diagonal(r, start, width, offset):
        s = _dot_nt(q_ref[rows(r), :], k_ref[pl.ds(start, width), :])
        s = jnp.where(_causal_mask(half, width, offset), s, NEG_BIG)
        m_old = m_ref[r]
        m_new = jnp.maximum(m_old, _row_max(s))
        p = jnp.exp2(s - m_new).astype(v_ref.dtype)
        acc_ref[r] = jnp.exp2(m_old - m_new) * acc_ref[r] + _dot(p, v_ref[pl.ds(start, width), :])
        m_ref[r] = m_new

    def qk(r, c):
        s_ref[r] = _dot_nt(q_ref[rows(r), :], k_ref[pl.ds(pl.multiple_of(c * tq, tq), tq), :])

    def softmax(r):
        s = s_ref[r]
        m_old = m_ref[r]
        m_new = jnp.maximum(m_old, _row_max(s))
        alpha_ref[r] = jnp.exp2(m_old - m_new)
        p_ref[r] = jnp.exp2(s - m_new).astype(p_ref.dtype)
        m_ref[r] = m_new

    def pv(r, c):
        v = v_ref[pl.ds(pl.multiple_of(c * tq, tq), tq), :]
        acc_ref[r] = alpha_ref[r] * acc_ref[r] + _dot(p_ref[r], v)

    d0 = pl.multiple_of(qi * tq, tq)

    def head():
        diagonal(0, d0, half, 0)
        diagonal(1, d0, tq, half)

    _two_stream_pipeline(qi, head, qk, softmax, pv)
    for r in range(2):
        acc = acc_ref[r]
        o_ref[rows(r), :] = (acc[:, :MLA_V_DIM] / acc[:, MLA_V_DIM:MLA_V_DIM + 1]).astype(o_ref.dtype)


def _mla_attention(qcat, kcat, vext, batch, seq, tq):
    m = qcat.shape[0]
    nh = MLA_HEADS
    nq = seq // tq
    w = 2 * LANES
    return pl.pallas_call(
        _mla_kernel,
        grid=(batch, nh, nq),
        in_specs=[pl.BlockSpec((tq, w), lambda b, h, qi: (b * nq + qi, h)),
                  pl.BlockSpec((seq, w), lambda b, h, qi: (b, h)),
                  pl.BlockSpec((seq, w), lambda b, h, qi: (b, h))],
        out_specs=pl.BlockSpec((tq, MLA_V_DIM), lambda b, h, qi: (b * nq + qi, h)),
        out_shape=jax.ShapeDtypeStruct((m, nh * MLA_V_DIM), BF16),
        scratch_shapes=[pltpu.VMEM((2, tq // 2, 1), F32),
                        pltpu.VMEM((2, tq // 2, 1), F32),
                        pltpu.VMEM((2, tq // 2, w), F32),
                        pltpu.VMEM((2, tq // 2, tq), F32),
                        pltpu.VMEM((2, tq // 2, tq), BF16)],
        compiler_params=_params(("parallel", "parallel", "arbitrary")),
        name="mla_attention",
    )(qcat, kcat, vext)


def _merge_kernel(x_ref, oa_ref, ob_ref, wga_ref, wgb_ref, wpa_ref, wpb_ref, ba_ref, bb_ref, o_ref):
    d = x_ref.shape[1]
    kc = oa_ref.shape[1] if d % oa_ref.shape[1] == 0 else d

    def gate(wt_ref, b_ref):
        acc = b_ref[...] + _dot_nt(x_ref[:, :kc], wt_ref[:, :kc])
        for c in range(kc, d, kc):
            acc = acc + _dot_nt(x_ref[:, c:c + kc], wt_ref[:, c:c + kc])
        return jax.nn.sigmoid(acc)

    ga = gate(wga_ref, ba_ref)
    gb = gate(wgb_ref, bb_ref)
    pa = _dot(oa_ref[...], wpa_ref[...])
    pb = _dot(ob_ref[...], wpb_ref[...])
    o_ref[...] = (ga * pa + gb * pb).astype(o_ref.dtype)


def _gated_merge(xb, oa, ob, wga_t, wgb_t, wpa, wpb, ba, bb, tm, tn):
    m, d = xb.shape
    n = wpa.shape[1]
    ko = oa.shape[1]
    return pl.pallas_call(
        _merge_kernel,
        grid=(m // tm, n // tn),
        in_specs=[pl.BlockSpec((tm, d), lambda i, j: (i, 0)),
                  pl.BlockSpec((tm, ko), lambda i, j: (i, 0)),
                  pl.BlockSpec((tm, ko), lambda i, j: (i, 0)),
                  pl.BlockSpec((tn, d), lambda i, j: (j, 0)),
                  pl.BlockSpec((tn, d), lambda i, j: (j, 0)),
                  pl.BlockSpec((ko, tn), lambda i, j: (0, j)),
                  pl.BlockSpec((ko, tn), lambda i, j: (0, j)),
                  pl.BlockSpec((1, tn), lambda i, j: (0, j)),
                  pl.BlockSpec((1, tn), lambda i, j: (0, j))],
        out_specs=pl.BlockSpec((tm, tn), lambda i, j: (i, j)),
        out_shape=jax.ShapeDtypeStruct((m, n), BF16),
        compiler_params=_params(("parallel", "arbitrary")),
        name="gated_merge",
    )(xb, oa, ob, wga_t, wgb_t, wpa, wpb, ba, bb)


def _mm_res_kernel(a_ref, w_ref, r_ref, o_ref, *, alpha):
    o_ref[...] = alpha * r_ref[...] + _dot(a_ref[...], w_ref[...])


def _mm_res(a, w, res, alpha, tm, tn):
    m, k = a.shape
    n = w.shape[1]
    return pl.pallas_call(
        functools.partial(_mm_res_kernel, alpha=alpha),
        grid=(m // tm, n // tn),
        in_specs=[pl.BlockSpec((tm, k), lambda i, j: (i, 0)),
                  pl.BlockSpec((k, tn), lambda i, j: (0, j)),
                  pl.BlockSpec((tm, tn), lambda i, j: (i, j))],
        out_specs=pl.BlockSpec((tm, tn), lambda i, j: (i, j)),
        out_shape=jax.ShapeDtypeStruct((m, n), F32),
        compiler_params=_params(("parallel", "arbitrary")),
        name="out_proj_residual",
    )(a, w, res)


def _layernorm(y, g, b):
    mu = jnp.mean(y, axis=-1, keepdims=True)
    yc = y - mu
    var = jnp.mean(yc * yc, axis=-1, keepdims=True)
    return yc * lax.rsqrt(var + LN_EPS) * g + b


def _ffn_up_kernel(y_ref, g_ref, b_ref, w_ref, cw_ref, cb_ref, o_ref, h_ref, u_ref, carry_ref, *, nj, tiles_per_seq):
    t = pl.program_id(0)
    last = pl.num_programs(0) - 1
    tm = y_ref.shape[0]
    half = o_ref.shape[1]
    pad = SUBLANES

    @pl.when(t == 0)
    def _():
        u_ref[...] = jnp.zeros(u_ref.shape, F32)
        carry_ref[...] = jnp.zeros(carry_ref.shape, F32)

    @pl.when((t % nj == 0) & (t < last))
    def _():
        h_ref[...] = _layernorm(y_ref[...], g_ref[...], b_ref[...]).astype(h_ref.dtype)

    tp = jnp.maximum(t - 1, 0)
    ip = tp // nj
    jp = tp % nj
    seq_start = (ip % tiles_per_seq == 0).astype(F32)
    u_ref[:pad, :] = carry_ref[jp] * (1.0 - seq_start)
    cw = cw_ref[...]
    c = cb_ref[...] + cw[2:3] * u_ref[pad:, :]
    for back in range(1, CONV_WIDTH):
        c = c + cw[CONV_WIDTH - 1 - back:CONV_WIDTH - back] * u_ref[pad - back:pad - back + tm, :]
    carry_ref[jp] = u_ref[tm:, :]
    gate = c[:, :half]
    o_ref[...] = (gate * jax.nn.sigmoid(gate) * c[:, half:]).astype(o_ref.dtype)

    u_ref[pad:, :] = _dot(h_ref[...], w_ref[...])


def _ffn_up(y, g, b, w_up, conv_w, conv_b, seq, tm, tn):
    m, d = y.shape
    n2 = w_up.shape[1]
    nj = n2 // (2 * tn)
    steps = (m // tm) * nj

    def cur(t):
        return jnp.minimum(t, steps - 1)

    def prev(t):
        return jnp.maximum(t - 1, 0)

    return pl.pallas_call(
        functools.partial(_ffn_up_kernel, nj=nj, tiles_per_seq=seq // tm),
        grid=(steps + 1,),
        in_specs=[pl.BlockSpec((tm, d), lambda t: (cur(t) // nj, 0)),
                  pl.BlockSpec((1, d), lambda t: (0, 0)),
                  pl.BlockSpec((1, d), lambda t: (0, 0)),
                  pl.BlockSpec((d, 2 * tn), lambda t: (0, cur(t) % nj)),
                  pl.BlockSpec((CONV_WIDTH, 2 * tn), lambda t: (0, prev(t) % nj)),
                  pl.BlockSpec((1, 2 * tn), lambda t: (0, prev(t) % nj))],
        out_specs=pl.BlockSpec((tm, tn), lambda t: (prev(t) // nj, prev(t) % nj)),
        out_shape=jax.ShapeDtypeStruct((m, n2 // 2), BF16),
        scratch_shapes=[pltpu.VMEM((tm, d), BF16),
                        pltpu.VMEM((tm + SUBLANES, 2 * tn), F32),
                        pltpu.VMEM((nj, SUBLANES, 2 * tn), F32)],
        compiler_params=_params(("arbitrary",)),
        name="ffn_up_conv_gate",
    )(y, g, b, w_up, conv_w, conv_b)


def _ffn_down_kernel(a_ref, w_ref, y_ref, g1_ref, b1_ref, g2_ref, b2_ref, o_ref, *, alpha):
    k = pl.program_id(1)
    tm, d = o_ref.shape

    @pl.when(k == 0)
    def _():
        o_ref[...] = jnp.zeros(o_ref.shape, F32)

    a = a_ref[...]
    for c in range(0, d, FFN_DOWN_COL_CHUNK):
        cols = slice(c, min(c + FFN_DOWN_COL_CHUNK, d))
        o_ref[:, cols] += _dot(a, w_ref[:, cols])

    @pl.when(k == pl.num_programs(1) - 1)
    def _():
        rows_per = min(tm, FFN_DOWN_ROW_CHUNK)

        def body(r, carry):
            rows = pl.ds(pl.multiple_of(r * rows_per, rows_per), rows_per)
            h = _layernorm(y_ref[rows, :], g1_ref[...], b1_ref[...])
            o_ref[rows, :] = _layernorm(alpha * h + o_ref[rows, :], g2_ref[...], b2_ref[...])
            return carry

        lax.fori_loop(0, tm // rows_per, body, 0)


def _ffn_down(act, w_down, y, g1, b1, g2, b2, alpha, tm, tk):
    m, kdim = act.shape
    d = w_down.shape[1]
    vec = pl.BlockSpec((1, d), lambda i, k: (0, 0))
    return pl.pallas_call(
        functools.partial(_ffn_down_kernel, alpha=alpha),
        grid=(m // tm, kdim // tk),
        in_specs=[pl.BlockSpec((tm, tk), lambda i, k: (i, k)),
                  pl.BlockSpec((tk, d), lambda i, k: (k, 0)),
                  pl.BlockSpec((tm, d), lambda i, k: (i, 0)),
                  vec, vec, vec, vec],
        out_specs=pl.BlockSpec((tm, d), lambda i, k: (i, 0)),
        out_shape=jax.ShapeDtypeStruct((m, d), F32),
        compiler_params=_params(("parallel", "arbitrary")),
        name="ffn_down_layernorm",
    )(act, w_down, y, g1, b1, g2, b2)


def _cast_kernel(x_ref, o_ref):
    o_ref[...] = x_ref[...].astype(o_ref.dtype)


def _cast_bf16(w, rb, cb, name):
    rows, cols = w.shape
    return pl.pallas_call(
        _cast_kernel,
        grid=(pl.cdiv(rows, rb), pl.cdiv(cols, cb)),
        in_specs=[pl.BlockSpec((rb, cb), lambda i, j: (i, j))],
        out_specs=pl.BlockSpec((rb, cb), lambda i, j: (i, j)),
        out_shape=jax.ShapeDtypeStruct((rows, cols), BF16),
        compiler_params=_params(("parallel", "parallel")),
        name=name,
    )(w)


def _regroup_kernel(src_ref, keep_ref, x_ref, o_ref):
    keep = keep_ref[pl.program_id(0)] > 0
    o_ref[...] = jnp.where(keep, x_ref[...], 0.0).astype(o_ref.dtype)


def _regroup_cast_w_up(w_up, f, tn, nj):
    d = w_up.shape[0]
    cb = math.gcd(f, tn)
    sub = tn // cb
    nsrc = f // cb
    src, keep = [], []
    for j in range(nj):
        for base in (0, nsrc):
            for r in range(sub):
                g = j * sub + r
                src.append(base + min(g, nsrc - 1))
                keep.append(1 if g < nsrc else 0)
    grid_spec = pltpu.PrefetchScalarGridSpec(
        num_scalar_prefetch=2,
        grid=(len(src),),
        in_specs=[pl.BlockSpec((d, cb), lambda b, src, keep: (0, src[b]))],
        out_specs=pl.BlockSpec((d, cb), lambda b, src, keep: (0, b)),
    )
    return pl.pallas_call(
        _regroup_kernel,
        grid_spec=grid_spec,
        out_shape=jax.ShapeDtypeStruct((d, nj * 2 * tn), BF16),
        compiler_params=_params(("parallel",)),
        name="w_up_regroup_cast",
    )(jnp.asarray(src, jnp.int32), jnp.asarray(keep, jnp.int32), w_up)


def _rope_rows(wt):
    half = MLA_ROPE_DIM // 2
    return jnp.concatenate([-wt[half:], wt[:half]], axis=0)


def _pad_rows(w, n):
    return jnp.pad(w, ((0, n - w.shape[0]), (0, 0)))


def _pad_cols(w, n):
    return jnp.pad(w, ((0, 0), (0, n - w.shape[1])))


def _layer(h, posb, pos_rows, batch, seq, alpha, lambda_init, w_in, b_gate, lam_p, da_g, q_g, kv_g, w_uq, w_ukv,
           w_proj_a, w_proj_b, w_out, ln1_g, ln1_b, w_up, conv_w, conv_b, w_down, ln2_g, ln2_b):
    m, d = h.shape
    f = w_down.shape[0]
    da_cols = 3 * DA_HEADS * DA_V_DIM
    off_cq = da_cols
    off_ckv = off_cq + MLA_Q_RANK
    off_kr = off_ckv + MLA_KV_RANK
    off_gate = off_kr + MLA_ROPE_DIM

    hb = h.astype(BF16)

    da_scale = DA_HEAD_DIM ** -0.5 * LOG2E
    qkv_scale = jnp.concatenate([jnp.full((DA_HEADS * DA_V_DIM,), da_scale, F32),
                                 jnp.ones((2 * DA_HEADS * DA_V_DIM,), F32)])[None, :]
    wt_in = _cast_bf16(jnp.swapaxes(w_in, 0, 1), 1024, _tile(d, 1024), "w_in_cast")
    qkv = _mm_scale(hb, wt_in, qkv_scale, BF16, _tile(m, 1024), 512, "da_qkv_proj")

    wt_kr = wt_in[off_kr:off_gate]
    wt_lat = jnp.concatenate([wt_in[off_cq:off_kr], _pad_rows(wt_kr, LANES), _pad_rows(_rope_rows(wt_kr), LANES)], axis=0)
    lat = _mm_scale(hb, wt_lat, jnp.ones((1, wt_lat.shape[0]), F32), F32, _tile(m, 1024), wt_lat.shape[0] // 2,
                    "mla_latent_proj")

    slopes = 2.0 ** (-8.0 * jnp.arange(1, DA_HEADS + 1, dtype=F32) / DA_HEADS)
    oa = _diff_attention(qkv, pos_rows, lam_p, da_g[None, :], slopes, batch, seq, _tile(seq, 512), lambda_init)

    half = MLA_ROPE_DIM // 2
    inv = ROPE_THETA ** (-jnp.arange(half, dtype=F32) / half)
    inv_row = _pad_cols(jnp.concatenate([inv, inv])[None, :], LANES)
    hd = MLA_NOPE_DIM + MLA_ROPE_DIM
    wq = w_uq.reshape(MLA_Q_RANK, MLA_HEADS, hd)
    zeros = jnp.zeros((MLA_Q_RANK, MLA_HEADS, LANES - MLA_ROPE_DIM), F32)
    wq_main = jnp.concatenate([wq, zeros], axis=2).reshape(MLA_Q_RANK, MLA_HEADS * 2 * LANES).astype(BF16)
    wq_swap = jnp.concatenate([-wq[:, :, MLA_NOPE_DIM + half:], wq[:, :, MLA_NOPE_DIM:MLA_NOPE_DIM + half], zeros],
                              axis=2).reshape(MLA_Q_RANK, MLA_HEADS * LANES).astype(BF16)
    qcat = _q_up(lat, q_g[None, :], posb, inv_row, wq_main, wq_swap, hd ** -0.5 * LOG2E, _tile(m, 1024))
    kcat, vext = _kv_up(lat, kv_g[None, :], posb, inv_row, w_ukv.astype(BF16), _tile(m, 512))
    ob = _mla_attention(qcat, kcat, vext, batch, seq, _tile(seq, 1024))

    merged = _gated_merge(hb, oa, ob, wt_in[off_gate:off_gate + d], wt_in[off_gate + d:],
                          w_proj_a.astype(BF16), w_proj_b.astype(BF16), b_gate[None, :d], b_gate[None, d:],
                          _tile(m, 512), _tile(d, 512))
    y1 = _mm_res(merged, w_out.astype(BF16), h, alpha, _tile(m, 1024), _tile(d, 512))

    tn = 512
    fp = -(-f // tn) * tn
    nj = fp // tn

    def group(a):
        r = a.shape[0]
        gv = jnp.stack([_pad_cols(a[:, :f], fp).reshape(r, nj, tn), _pad_cols(a[:, f:], fp).reshape(r, nj, tn)], axis=2)
        return gv.reshape(r, nj * 2 * tn)

    act = _ffn_up(y1, ln1_g[None, :], ln1_b[None, :], _regroup_cast_w_up(w_up, f, tn, nj), group(conv_w),
                  group(conv_b[None, :]), seq, _tile(seq, 512), tn)
    w_down_p = jnp.pad(w_down.astype(BF16), ((0, fp - f), (0, 0)))
    return _ffn_down(act, w_down_p, y1, ln1_g[None, :], ln1_b[None, :], ln2_g[None, :], ln2_b[None, :], alpha,
                     _tile(m, 512), _tile(fp, 1024))


def kernel(x, positions, w_in, b_gate, da_lambda_q1, da_lambda_k1, da_lambda_q2, da_lambda_k2, da_subln_g, mla_q_norm_g, mla_kv_norm_g, w_uq, w_ukv, w_proj_a, w_proj_b, w_out, ln1_g, ln1_b, w_up, conv_w, conv_b, w_down, ln2_g, ln2_b):
    batch, seq, d = x.shape
    depth = w_in.shape[0]
    alpha = (2.0 * depth) ** 0.25
    m = batch * seq
    posf = positions.astype(F32).reshape(m)
    posb = jnp.broadcast_to(posf[:, None], (m, LANES))
    pos_rows = posf.reshape(batch, 1, seq)
    h = x.reshape(m, d)
    for l in range(depth):
        lambda_init = 0.8 - 0.6 * math.exp(-0.3 * l)
        lam_p = jnp.stack([da_lambda_q1[l], da_lambda_k1[l], da_lambda_q2[l], da_lambda_k2[l]]).astype(F32)
        h = _layer(h, posb, pos_rows, batch, seq, alpha, lambda_init, w_in[l], b_gate[l], lam_p, da_subln_g[l],
                   mla_q_norm_g[l], mla_kv_norm_g[l], w_uq[l], w_ukv[l], w_proj_a[l], w_proj_b[l], w_out[l],
                   ln1_g[l], ln1_b[l], w_up[l], conv_w[l], conv_b[l], w_down[l], ln2_g[l], ln2_b[l])
    return h.reshape(batch, seq, d)
```

```python
import functools
import math
from typing import NamedTuple

import numpy as np
import jax
import jax.numpy as jnp
from jax import lax
from jax.experimental import pallas as pl
from jax.experimental.pallas import tpu as pltpu

DA_HEADS = 8
DA_HEAD_DIM = 128
DA_V_DIM = 2 * DA_HEAD_DIM
MLA_HEADS = 16
MLA_Q_RANK = 1024
MLA_KV_RANK = 512
MLA_NOPE_DIM = 128
MLA_ROPE_DIM = 64
MLA_V_DIM = 128
ROPE_THETA = 10000.0
CONV_WIDTH = 3
LN_EPS = 1e-5
RMS_EPS = 1e-6

LANES = 128
SUBLANES = 8
VMEM_LIMIT = 56 * 1024 * 1024
NEG_BIG = -1e30
LOG2E = math.log2(math.e)
Q_UP_HEADS_PER_STEP = 4
FFN_COLS_PER_STEP = 512
FFN_DOWN_COL_CHUNK = 1024
FFN_DOWN_ROW_CHUNK = 64

F32 = jnp.float32
BF16 = jnp.bfloat16


def _params(sem):
    return pltpu.CompilerParams(dimension_semantics=sem, vmem_limit_bytes=VMEM_LIMIT)


def _tile(n, pref):
    t = min(n, pref)
    while n % t:
        t //= 2
    return t


def _dot(a, b):
    return jnp.dot(a, b, preferred_element_type=F32)


def _dot_nt(a, b):
    return lax.dot_general(a, b, (((1,), (1,)), ((), ())), preferred_element_type=F32)


def _mm_scale_kernel(a_ref, wt_ref, s_ref, o_ref):
    o_ref[...] = (_dot_nt(a_ref[...], wt_ref[...]) * s_ref[...]).astype(o_ref.dtype)


def _mm_scale(a, wt, scale, out_dtype, tm, tn, name):
    m, k = a.shape
    n = scale.shape[1]
    return pl.pallas_call(
        _mm_scale_kernel,
        grid=(m // tm, n // tn),
        in_specs=[pl.BlockSpec((tm, k), lambda i, j: (i, 0)),
                  pl.BlockSpec((tn, k), lambda i, j: (j, 0)),
                  pl.BlockSpec((1, tn), lambda i, j: (0, j))],
        out_specs=pl.BlockSpec((tm, tn), lambda i, j: (i, j)),
        out_shape=jax.ShapeDtypeStruct((m, n), out_dtype),
        compiler_params=_params(("parallel", "arbitrary")),
        name=name,
    )(a, wt, scale)


def _rms_bf16(c, g):
    ms = jnp.mean(c * c, axis=-1, keepdims=True)
    return (c * lax.rsqrt(ms + RMS_EPS) * g).astype(BF16)


def _qup_kernel(cq_ref, g_ref, pos_ref, inv_ref, wm_ref, ws_ref, o_ref, cn_ref, cos_ref, sin_ref, *, scale):
    @pl.when(pl.program_id(1) == 0)
    def _():
        cn_ref[...] = _rms_bf16(cq_ref[...], g_ref[...])
        ang = pos_ref[...] * inv_ref[...]
        cos_ref[...] = jnp.cos(ang)
        sin_ref[...] = jnp.sin(ang)

    cn = cn_ref[...]
    main = _dot(cn, wm_ref[...])
    swapped = _dot(cn, ws_ref[...])
    for hh in range(o_ref.shape[1] // (2 * LANES)):
        c0 = hh * 2 * LANES
        rope = main[:, c0 + LANES:c0 + 2 * LANES] * cos_ref[...] + swapped[:, hh * LANES:(hh + 1) * LANES] * sin_ref[...]
        o_ref[:, c0:c0 + LANES] = (main[:, c0:c0 + LANES] * scale).astype(o_ref.dtype)
        o_ref[:, c0 + LANES:c0 + 2 * LANES] = (rope * scale).astype(o_ref.dtype)


def _q_up(lat, g, posb, inv_row, w_main, w_swap, scale, tm):
    m = lat.shape[0]
    nh = MLA_HEADS
    hg = Q_UP_HEADS_PER_STEP
    return pl.pallas_call(
        functools.partial(_qup_kernel, scale=scale),
        grid=(m // tm, nh // hg),
        in_specs=[pl.BlockSpec((tm, MLA_Q_RANK), lambda i, h: (i, 0)),
                  pl.BlockSpec((1, MLA_Q_RANK), lambda i, h: (0, 0)),
                  pl.BlockSpec((tm, LANES), lambda i, h: (i, 0)),
                  pl.BlockSpec((1, LANES), lambda i, h: (0, 0)),
                  pl.BlockSpec((MLA_Q_RANK, hg * 2 * LANES), lambda i, h: (0, h)),
                  pl.BlockSpec((MLA_Q_RANK, hg * LANES), lambda i, h: (0, h))],
        out_specs=pl.BlockSpec((tm, hg * 2 * LANES), lambda i, h: (i, h)),
        out_shape=jax.ShapeDtypeStruct((m, nh * 2 * LANES), BF16),
        scratch_shapes=[pltpu.VMEM((tm, MLA_Q_RANK), BF16),
                        pltpu.VMEM((tm, LANES), F32),
                        pltpu.VMEM((tm, LANES), F32)],
        compiler_params=_params(("parallel", "arbitrary")),
        name="mla_q_up",
    )(lat, g, posb, inv_row, w_main, w_swap)


def _kvup_kernel(ckv_ref, kr_ref, krs_ref, g_ref, pos_ref, inv_ref, w_ref, k_ref, v_ref):
    tm = ckv_ref.shape[0]
    cn = _rms_bf16(ckv_ref[...], g_ref[...])
    ang = pos_ref[...] * inv_ref[...]
    krope = (kr_ref[...] * jnp.cos(ang) + krs_ref[...] * jnp.sin(ang)).astype(k_ref.dtype)
    ones_col = (lax.broadcasted_iota(jnp.int32, (tm, LANES), 1) == 0).astype(v_ref.dtype)
    wh = MLA_NOPE_DIM + MLA_V_DIM
    for h in range(MLA_HEADS):
        kv = _dot(cn, w_ref[:, h * wh:(h + 1) * wh])
        k_ref[:, h * wh:h * wh + MLA_NOPE_DIM] = kv[:, :MLA_NOPE_DIM].astype(k_ref.dtype)
        k_ref[:, h * wh + MLA_NOPE_DIM:(h + 1) * wh] = krope
        v_ref[:, h * wh:h * wh + MLA_V_DIM] = kv[:, MLA_NOPE_DIM:].astype(v_ref.dtype)
        v_ref[:, h * wh + MLA_V_DIM:(h + 1) * wh] = ones_col


def _kv_up(lat, g, posb, inv_row, w_ukv, tm):
    m = lat.shape[0]
    n = w_ukv.shape[1]
    ckv_blk = MLA_Q_RANK // MLA_KV_RANK
    kr_blk = (MLA_Q_RANK + MLA_KV_RANK) // LANES
    return pl.pallas_call(
        _kvup_kernel,
        grid=(m // tm,),
        in_specs=[pl.BlockSpec((tm, MLA_KV_RANK), lambda i: (i, ckv_blk)),
                  pl.BlockSpec((tm, LANES), lambda i: (i, kr_blk)),
                  pl.BlockSpec((tm, LANES), lambda i: (i, kr_blk + 1)),
                  pl.BlockSpec((1, MLA_KV_RANK), lambda i: (0, 0)),
                  pl.BlockSpec((tm, LANES), lambda i: (i, 0)),
                  pl.BlockSpec((1, LANES), lambda i: (0, 0)),
                  pl.BlockSpec((MLA_KV_RANK, n), lambda i: (0, 0))],
        out_specs=[pl.BlockSpec((tm, n), lambda i: (i, 0)),
                   pl.BlockSpec((tm, n), lambda i: (i, 0))],
        out_shape=[jax.ShapeDtypeStruct((m, n), BF16),
                   jax.ShapeDtypeStruct((m, n), BF16)],
        compiler_params=_params(("parallel",)),
        name="mla_kv_up",
    )(lat, lat, lat, g, posb, inv_row, w_ukv)


def _row_max(s):
    part = s[:, :LANES]
    for c in range(LANES, s.shape[1], LANES):
        part = jnp.maximum(part, s[:, c:c + LANES])
    return jnp.max(part, axis=1, keepdims=True)


def _lane_partial_sum(p):
    part = p[:, :LANES]
    for c in range(LANES, p.shape[1], LANES):
        part = part + p[:, c:c + LANES]
    return part


def _causal_mask(rows, cols, offset):
    row = lax.broadcasted_iota(jnp.int32, (rows, cols), 0)
    col = lax.broadcasted_iota(jnp.int32, (rows, cols), 1)
    return col <= row + offset


def _da_kernel(slope_ref, q_ref, k_ref, v_ref, pos_ref, lam_ref, g_ref, o_ref,
               m_ref, alpha_ref, l_ref, acc_ref, s_ref, p_ref, *, lambda_init):
    h = pl.program_id(1)
    qi = pl.program_id(2)
    d = DA_HEAD_DIM
    tq = q_ref.shape[0]
    tk = s_ref.shape[2]

    m_ref[...] = jnp.full(m_ref.shape, NEG_BIG, F32)
    l_ref[...] = jnp.zeros(l_ref.shape, F32)
    acc_ref[...] = jnp.zeros(acc_ref.shape, F32)

    slope2 = slope_ref[h] * LOG2E
    q0 = pl.multiple_of(qi * tq, tq)
    pq0 = pos_ref[:, pl.ds(q0, LANES)][:, :1]

    def scores(r, rows, start, width):
        cols = slice(r * d, (r + 1) * d)
        bias = slope2 * (pos_ref[:, pl.ds(start, width)] - pq0)
        return _dot_nt(q_ref[rows, cols], k_ref[pl.ds(start, width), cols]) + bias

    def diagonal():
        half = tq // 2
        for rows, width, offset in ((slice(0, half), half, 0), (slice(half, tq), tq, half)):
            v = v_ref[pl.ds(q0, width), :]
            for r in range(2):
                s = jnp.where(_causal_mask(half, width, offset), scores(r, rows, q0, width), NEG_BIG)
                m_old = m_ref[r, rows, :]
                m_new = jnp.maximum(m_old, _row_max(s))
                alpha = jnp.exp2(m_old - m_new)
                p = jnp.exp2(s - m_new)
                l_ref[r, rows, :] = alpha * l_ref[r, rows, :] + _lane_partial_sum(p)
                acc_ref[r, rows, :] = alpha * acc_ref[r, rows, :] + _dot(p.astype(v.dtype), v)
                m_ref[r, rows, :] = m_new

    def qk(r, c):
        s_ref[r] = scores(r, slice(None), pl.multiple_of(c * tk, tk), tk)

    def softmax(r):
        s = s_ref[r]
        m_old = m_ref[r]
        m_new = jnp.maximum(m_old, _row_max(s))
        alpha = jnp.exp2(m_old - m_new)
        p = jnp.exp2(s - m_new)
        alpha_ref[r] = alpha
        l_ref[r] = alpha * l_ref[r] + _lane_partial_sum(p)
        p_ref[r] = p.astype(p_ref.dtype)
        m_ref[r] = m_new

    def pv(r, c):
        v = v_ref[pl.ds(pl.multiple_of(c * tk, tk), tk), :]
        acc_ref[r] = alpha_ref[r] * acc_ref[r] + _dot(p_ref[r], v)

    _two_stream_pipeline(qi, diagonal, qk, softmax, pv)

    lam_p = lam_ref[...]
    lam = (jnp.exp(jnp.sum(lam_p[0:1] * lam_p[1:2], axis=1, keepdims=True))
           - jnp.exp(jnp.sum(lam_p[2:3] * lam_p[3:4], axis=1, keepdims=True)) + lambda_init)
    l0 = jnp.sum(l_ref[0], axis=1, keepdims=True)
    l1 = jnp.sum(l_ref[1], axis=1, keepdims=True)
    o = acc_ref[0] / l0 - lam * (acc_ref[1] / l1)
    ms = jnp.mean(o * o, axis=-1, keepdims=True)
    o_ref[...] = (o * lax.rsqrt(ms + RMS_EPS) * g_ref[...] * (1.0 - lambda_init)).astype(o_ref.dtype)


def _diff_attention(qkv, pos_rows, lam_p, g, slopes, batch, seq, tq, lambda_init):
    m = qkv.shape[0]
    nh = DA_HEADS
    nq = seq // tq
    w = DA_V_DIM
    grid_spec = pltpu.PrefetchScalarGridSpec(
        num_scalar_prefetch=1,
        grid=(batch, nh, nq),
        in_specs=[pl.BlockSpec((tq, w), lambda b, h, qi, sl: (b * nq + qi, h)),
                  pl.BlockSpec((seq, w), lambda b, h, qi, sl: (b, nh + h)),
                  pl.BlockSpec((seq, w), lambda b, h, qi, sl: (b, 2 * nh + h)),
                  pl.BlockSpec((None, 1, seq), lambda b, h, qi, sl: (b, 0, 0)),
                  pl.BlockSpec((4, DA_HEAD_DIM), lambda b, h, qi, sl: (0, 0)),
                  pl.BlockSpec((1, w), lambda b, h, qi, sl: (0, 0))],
        out_specs=pl.BlockSpec((tq, w), lambda b, h, qi, sl: (b * nq + qi, h)),
        scratch_shapes=[pltpu.VMEM((2, tq, 1), F32),
                        pltpu.VMEM((2, tq, 1), F32),
                        pltpu.VMEM((2, tq, LANES), F32),
                        pltpu.VMEM((2, tq, w), F32),
                        pltpu.VMEM((2, tq, tq), F32),
                        pltpu.VMEM((2, tq, tq), BF16)],
    )
    return pl.pallas_call(
        functools.partial(_da_kernel, lambda_init=lambda_init),
        grid_spec=grid_spec,
        out_shape=jax.ShapeDtypeStruct((m, nh * w), BF16),
        compiler_params=_params(("parallel", "parallel", "arbitrary")),
        name="diff_attention",
    )(slopes, qkv, qkv, qkv, pos_rows, lam_p, g)


def _two_stream_pipeline(n, head, qk, softmax, pv):
    @pl.when(n == 0)
    def _():
        head()

    @pl.when(n > 0)
    def _():
        head()
        qk(0, 0)
        softmax(0)
        qk(1, 0)

        def trip(c, carry):
            pv(0, c - 1)
            softmax(1)
            qk(0, c)
            pv(1, c - 1)
            softmax(0)
            qk(1, c)
            return carry

        lax.fori_loop(1, n, trip, 0)
        pv(0, n - 1)
        softmax(1)
        pv(1, n - 1)


def _mla_kernel(q_ref, k_ref, v_ref, o_ref, m_ref, alpha_ref, acc_ref, s_ref, p_ref):
    qi = pl.program_id(2)
    tq = q_ref.shape[0]
    half = tq // 2

    m_ref[...] = jnp.full(m_ref.shape, NEG_BIG, F32)
    acc_ref[...] = jnp.zeros(acc_ref.shape, F32)

    def rows(r):
        return slice(r * half, (r + 1) * half)

    def diagonal(r, start, width, offset):
        s = _dot_nt(q_ref[rows(r), :], k_ref[pl.ds(start, width), :])
        s = jnp.where(_causal_mask(half, width, offset), s, NEG_BIG)
        m_old = m_ref[r]
        m_new = jnp.maximum(m_old, _row_max(s))
        p = jnp.exp2(s - m_new).astype(v_ref.dtype)
        acc_ref[r] = jnp.exp2(m_old - m_new) * acc_ref[r] + _dot(p, v_ref[pl.ds(start, width), :])
        m_ref[r] = m_new

    def qk(r, c):
        s_ref[r] = _dot_nt(q_ref[rows(r), :], k_ref[pl.ds(pl.multiple_of(c * tq, tq), tq), :])

    def softmax(r):
        s = s_ref[r]
        m_old = m_ref[r]
        m_new = jnp.maximum(m_old, _row_max(s))
        alpha_ref[r] = jnp.exp2(m_old - m_new)
        p_ref[r] = jnp.exp2(s - m_new).astype(p_ref.dtype)
        m_ref[r] = m_new

    def pv(r, c):
        v = v_ref[pl.ds(pl.multiple_of(c * tq, tq), tq), :]
        acc_ref[r] = alpha_ref[r] * acc_ref[r] + _dot(p_ref[r], v)

    d0 = pl.multiple_of(qi * tq, tq)

    def head():
        diagonal(0, d0, half, 0)
        diagonal(1, d0, tq, half)

    _two_stream_pipeline(qi, head, qk, softmax, pv)
    for r in range(2):
        acc = acc_ref[r]
        o_ref[rows(r), :] = (acc[:, :MLA_V_DIM] / acc[:, MLA_V_DIM:MLA_V_DIM + 1]).astype(o_ref.dtype)


def _mla_attention(qcat, kcat, vext, batch, seq, tq):
    m = qcat.shape[0]
    nh = MLA_HEADS
    nq = seq // tq
    w = 2 * LANES
    return pl.pallas_call(
        _mla_kernel,
        grid=(batch, nh, nq),
        in_specs=[pl.BlockSpec((tq, w), lambda b, h, qi: (b * nq + qi, h)),
                  pl.BlockSpec((seq, w), lambda b, h, qi: (b, h)),
                  pl.BlockSpec((seq, w), lambda b, h, qi: (b, h))],
        out_specs=pl.BlockSpec((tq, MLA_V_DIM), lambda b, h, qi: (b * nq + qi, h)),
        out_shape=jax.ShapeDtypeStruct((m, nh * MLA_V_DIM), BF16),
        scratch_shapes=[pltpu.VMEM((2, tq // 2, 1), F32),
                        pltpu.VMEM((2, tq // 2, 1), F32),
                        pltpu.VMEM((2, tq // 2, w), F32),
                        pltpu.VMEM((2, tq // 2, tq), F32),
                        pltpu.VMEM((2, tq // 2, tq), BF16)],
        compiler_params=_params(("parallel", "parallel", "arbitrary")),
        name="mla_attention",
    )(qcat, kcat, vext)


def _merge_kernel(x_ref, oa_ref, ob_ref, wga_ref, wgb_ref, wpa_ref, wpb_ref, ba_ref, bb_ref, o_ref):
    d = x_ref.shape[1]
    kc = oa_ref.shape[1] if d % oa_ref.shape[1] == 0 else d

    def gate(wt_ref, b_ref):
        acc = b_ref[...] + _dot_nt(x_ref[:, :kc], wt_ref[:, :kc])
        for c in range(kc, d, kc):
            acc = acc + _dot_nt(x_ref[:, c:c + kc], wt_ref[:, c:c + kc])
        return jax.nn.sigmoid(acc)

    ga = gate(wga_ref, ba_ref)
    gb = gate(wgb_ref, bb_ref)
    pa = _dot(oa_ref[...], wpa_ref[...])
    pb = _dot(ob_ref[...], wpb_ref[...])
    o_ref[...] = (ga * pa + gb * pb).astype(o_ref.dtype)


def _gated_merge(xb, oa, ob, wga_t, wgb_t, wpa, wpb, ba, bb, tm, tn):
    m, d = xb.shape
    n = wpa.shape[1]
    ko = oa.shape[1]
    return pl.pallas_call(
        _merge_kernel,
        grid=(m // tm, n // tn),
        in_specs=[pl.BlockSpec((tm, d), lambda i, j: (i, 0)),
                  pl.BlockSpec((tm, ko), lambda i, j: (i, 0)),
                  pl.BlockSpec((tm, ko), lambda i, j: (i, 0)),
                  pl.BlockSpec((tn, d), lambda i, j: (j, 0)),
                  pl.BlockSpec((tn, d), lambda i, j: (j, 0)),
                  pl.BlockSpec((ko, tn), lambda i, j: (0, j)),
                  pl.BlockSpec((ko, tn), lambda i, j: (0, j)),
                  pl.BlockSpec((1, tn), lambda i, j: (0, j)),
                  pl.BlockSpec((1, tn), lambda i, j: (0, j))],
        out_specs=pl.BlockSpec((tm, tn), lambda i, j: (i, j)),
        out_shape=jax.ShapeDtypeStruct((m, n), BF16),
        compiler_params=_params(("parallel", "arbitrary")),
        name="gated_merge",
    )(xb, oa, ob, wga_t, wgb_t, wpa, wpb, ba, bb)


def _mm_res_kernel(a_ref, w_ref, r_ref, o_ref, *, alpha):
    o_ref[...] = alpha * r_ref[...] + _dot(a_ref[...], w_ref[...])


def _mm_res(a, w, res, alpha, tm, tn):
    m, k = a.shape
    n = w.shape[1]
    return pl.pallas_call(
        functools.partial(_mm_res_kernel, alpha=alpha),
        grid=(m // tm, n // tn),
        in_specs=[pl.BlockSpec((tm, k), lambda i, j: (i, 0)),
                  pl.BlockSpec((k, tn), lambda i, j: (0, j)),
                  pl.BlockSpec((tm, tn), lambda i, j: (i, j))],
        out_specs=pl.BlockSpec((tm, tn), lambda i, j: (i, j)),
        out_shape=jax.ShapeDtypeStruct((m, n), F32),
        compiler_params=_params(("parallel", "arbitrary")),
        name="out_proj_residual",
    )(a, w, res)


def _layernorm(y, g, b):
    mu = jnp.mean(y, axis=-1, keepdims=True)
    yc = y - mu
    var = jnp.mean(yc * yc, axis=-1, keepdims=True)
    return yc * lax.rsqrt(var + LN_EPS) * g + b


def _ffn_up_kernel(y_ref, g_ref, b_ref, w_ref, cw_ref, cb_ref, o_ref, h_ref, u_ref, carry_ref, *, nj, tiles_per_seq):
    t = pl.program_id(0)
    last = pl.num_programs(0) - 1
    tm = y_ref.shape[0]
    half = o_ref.shape[1]
    pad = SUBLANES

    @pl.when(t == 0)
    def _():
        u_ref[...] = jnp.zeros(u_ref.shape, F32)
        carry_ref[...] = jnp.zeros(carry_ref.shape, F32)

    @pl.when((t % nj == 0) & (t < last))
    def _():
        h_ref[...] = _layernorm(y_ref[...], g_ref[...], b_ref[...]).astype(h_ref.dtype)

    tp = jnp.maximum(t - 1, 0)
    ip = tp // nj
    jp = tp % nj
    seq_start = (ip % tiles_per_seq == 0).astype(F32)
    u_ref[:pad, :] = carry_ref[jp] * (1.0 - seq_start)
    cw = cw_ref[...]
    c = cb_ref[...] + cw[2:3] * u_ref[pad:, :]
    for back in range(1, CONV_WIDTH):
        c = c + cw[CONV_WIDTH - 1 - back:CONV_WIDTH - back] * u_ref[pad - back:pad - back + tm, :]
    carry_ref[jp] = u_ref[tm:, :]
    gate = c[:, :half]
    o_ref[...] = (gate * jax.nn.sigmoid(gate) * c[:, half:]).astype(o_ref.dtype)

    u_ref[pad:, :] = _dot(h_ref[...], w_ref[...])


def _ffn_up(y, g, b, w_up, conv_w, conv_b, seq, tm, tn):
    m, d = y.shape
    n2 = w_up.shape[1]
    nj = n2 // (2 * tn)
    steps = (m // tm) * nj

    def cur(t):
        return jnp.minimum(t, steps - 1)

    def prev(t):
        return jnp.maximum(t - 1, 0)

    return pl.pallas_call(
        functools.partial(_ffn_up_kernel, nj=nj, tiles_per_seq=seq // tm),
        grid=(steps + 1,),
        in_specs=[pl.BlockSpec((tm, d), lambda t: (cur(t) // nj, 0)),
                  pl.BlockSpec((1, d), lambda t: (0, 0)),
                  pl.BlockSpec((1, d), lambda t: (0, 0)),
                  pl.BlockSpec((d, 2 * tn), lambda t: (0, cur(t) % nj)),
                  pl.BlockSpec((CONV_WIDTH, 2 * tn), lambda t: (0, prev(t) % nj)),
                  pl.BlockSpec((1, 2 * tn), lambda t: (0, prev(t) % nj))],
        out_specs=pl.BlockSpec((tm, tn), lambda t: (prev(t) // nj, prev(t) % nj)),
        out_shape=jax.ShapeDtypeStruct((m, n2 // 2), BF16),
        scratch_shapes=[pltpu.VMEM((tm, d), BF16),
                        pltpu.VMEM((tm + SUBLANES, 2 * tn), F32),
                        pltpu.VMEM((nj, SUBLANES, 2 * tn), F32)],
        compiler_params=_params(("arbitrary",)),
        name="ffn_up_conv_gate",
    )(y, g, b, w_up, conv_w, conv_b)


def _ffn_down_kernel(a_ref, w_ref, y_ref, g1_ref, b1_ref, g2_ref, b2_ref, o_ref, *, alpha):
    k = pl.program_id(1)
    tm, d = o_ref.shape

    @pl.when(k == 0)
    def _():
        o_ref[...] = jnp.zeros(o_ref.shape, F32)

    a = a_ref[...]
    for c in range(0, d, FFN_DOWN_COL_CHUNK):
        cols = slice(c, min(c + FFN_DOWN_COL_CHUNK, d))
        o_ref[:, cols] += _dot(a, w_ref[:, cols])

    @pl.when(k == pl.num_programs(1) - 1)
    def _():
        rows_per = min(tm, FFN_DOWN_ROW_CHUNK)

        def body(r, carry):
            rows = pl.ds(pl.multiple_of(r * rows_per, rows_per), rows_per)
            h = _layernorm(y_ref[rows, :], g1_ref[...], b1_ref[...])
            o_ref[rows, :] = _layernorm(alpha * h + o_ref[rows, :], g2_ref[...], b2_ref[...])
            return carry

        lax.fori_loop(0, tm // rows_per, body, 0)


def _ffn_down(act, w_down, y, g1, b1, g2, b2, alpha, tm, tk):
    m, kdim = act.shape
    d = w_down.shape[1]
    vec = pl.BlockSpec((1, d), lambda i, k: (0, 0))
    return pl.pallas_call(
        functools.partial(_ffn_down_kernel, alpha=alpha),
        grid=(m // tm, kdim // tk),
        in_specs=[pl.BlockSpec((tm, tk), lambda i, k: (i, k)),
                  pl.BlockSpec((tk, d), lambda i, k: (k, 0)),
                  pl.BlockSpec((tm, d), lambda i, k: (i, 0)),
                  vec, vec, vec, vec],
        out_specs=pl.BlockSpec((tm, d), lambda i, k: (i, 0)),
        out_shape=jax.ShapeDtypeStruct((m, d), F32),
        compiler_params=_params(("parallel", "arbitrary")),
        name="ffn_down_layernorm",
    )(act, w_down, y, g1, b1, g2, b2)


def _cast_kernel(x_ref, o_ref):
    o_ref[...] = x_ref[...].astype(o_ref.dtype)


def _cast_bf16(w, rb, cb, name):
    rows, cols = w.shape
    return pl.pallas_call(
        _cast_kernel,
        grid=(pl.cdiv(rows, rb), pl.cdiv(cols, cb)),
        in_specs=[pl.BlockSpec((rb, cb), lambda i, j: (i, j))],
        out_specs=pl.BlockSpec((rb, cb), lambda i, j: (i, j)),
        out_shape=jax.ShapeDtypeStruct((rows, cols), BF16),
        compiler_params=_params(("parallel", "parallel")),
        name=name,
    )(w)


def _cast_pad_kernel(x_ref, o_ref, *, n_src):
    keep = pl.program_id(0) < n_src
    o_ref[...] = jnp.where(keep, x_ref[...], 0.0).astype(o_ref.dtype)


def _cast_pad_rows_bf16(w, rows_out, rb, name):
    rows, cols = w.shape
    n_src = rows // rb
    return pl.pallas_call(
        functools.partial(_cast_pad_kernel, n_src=n_src),
        grid=(rows_out // rb,),
        in_specs=[pl.BlockSpec((rb, cols), lambda i: (jnp.minimum(i, n_src - 1), 0))],
        out_specs=pl.BlockSpec((rb, cols), lambda i: (i, 0)),
        out_shape=jax.ShapeDtypeStruct((rows_out, cols), BF16),
        compiler_params=_params(("parallel",)),
        name=name,
    )(w)


def _regroup_kernel(src_ref, keep_ref, x_ref, o_ref):
    keep = keep_ref[pl.program_id(0)] > 0
    o_ref[...] = jnp.where(keep, x_ref[...], 0.0).astype(o_ref.dtype)


def _regroup_cast_w_up(w_up, f, tn, nj):
    d = w_up.shape[0]
    cb = math.gcd(f, tn)
    sub = tn // cb
    nsrc = f // cb
    src, keep = [], []
    for j in range(nj):
        for base in (0, nsrc):
            for r in range(sub):
                g = j * sub + r
                src.append(base + min(g, nsrc - 1))
                keep.append(1 if g < nsrc else 0)
    grid_spec = pltpu.PrefetchScalarGridSpec(
        num_scalar_prefetch=2,
        grid=(len(src),),
        in_specs=[pl.BlockSpec((d, cb), lambda b, src, keep: (0, src[b]))],
        out_specs=pl.BlockSpec((d, cb), lambda b, src, keep: (0, b)),
    )
    return pl.pallas_call(
        _regroup_kernel,
        grid_spec=grid_spec,
        out_shape=jax.ShapeDtypeStruct((d, nj * 2 * tn), BF16),
        compiler_params=_params(("parallel",)),
        name="w_up_regroup_cast",
    )(jnp.asarray(src, jnp.int32), jnp.asarray(keep, jnp.int32), w_up)


def _rope_rows(wt):
    half = MLA_ROPE_DIM // 2
    return jnp.concatenate([-wt[half:], wt[:half]], axis=0)


def _pad_rows(w, n):
    return jnp.pad(w, ((0, n - w.shape[0]), (0, 0)))


def _pad_cols(w, n):
    return jnp.pad(w, ((0, 0), (0, n - w.shape[1])))


class _Tiles(NamedTuple):
    proj_m: int
    proj_n: int
    kv_up_m: int
    attn_q: int
    merge_m: int
    merge_n: int
    ffn_m: int
    ffn_n: int
    ffn_k: int
    cast: int


def _plan_tiles(m, d, seq, fp):
    return _Tiles(proj_m=_tile(m, 1024), proj_n=_tile(d, 512), kv_up_m=_tile(m, 512), attn_q=_tile(seq, 1024),
                  merge_m=_tile(m, 512), merge_n=_tile(d, 512), ffn_m=_tile(seq, 512), ffn_n=FFN_COLS_PER_STEP,
                  ffn_k=_tile(fp, 1024), cast=1024)


def _layer(h, posb, pos_rows, batch, seq, alpha, lambda_init, w_in, b_gate, lam_p, da_g, q_g, kv_g, w_uq, w_ukv,
           w_proj_a, w_proj_b, w_out, ln1_g, ln1_b, w_up, conv_w, conv_b, w_down, ln2_g, ln2_b):
    m, d = h.shape
    f = w_down.shape[0]
    fp = -(-f // FFN_COLS_PER_STEP) * FFN_COLS_PER_STEP
    t = _plan_tiles(m, d, seq, fp)
    da_cols = 3 * DA_HEADS * DA_V_DIM
    off_cq = da_cols
    off_ckv = off_cq + MLA_Q_RANK
    off_kr = off_ckv + MLA_KV_RANK
    off_gate = off_kr + MLA_ROPE_DIM

    hb = h.astype(BF16)

    da_scale = DA_HEAD_DIM ** -0.5 * LOG2E
    qkv_scale = jnp.concatenate([jnp.full((DA_HEADS * DA_V_DIM,), da_scale, F32),
                                 jnp.ones((2 * DA_HEADS * DA_V_DIM,), F32)])[None, :]
    wt_in = _cast_bf16(jnp.swapaxes(w_in, 0, 1), t.cast, _tile(d, t.cast), "w_in_cast")
    qkv = _mm_scale(hb, wt_in, qkv_scale, BF16, t.proj_m, t.proj_n, "da_qkv_proj")

    wt_kr = wt_in[off_kr:off_gate]
    wt_lat = jnp.concatenate([wt_in[off_cq:off_kr], _pad_rows(wt_kr, LANES), _pad_rows(_rope_rows(wt_kr), LANES)], axis=0)
    lat = _mm_scale(hb, wt_lat, jnp.ones((1, wt_lat.shape[0]), F32), F32, t.proj_m, wt_lat.shape[0] // 2,
                    "mla_latent_proj")

    slopes = 2.0 ** (-8.0 * jnp.arange(1, DA_HEADS + 1, dtype=F32) / DA_HEADS)
    oa = _diff_attention(qkv, pos_rows, lam_p, da_g[None, :], slopes, batch, seq, t.attn_q, lambda_init)

    half = MLA_ROPE_DIM // 2
    inv = ROPE_THETA ** (-jnp.arange(half, dtype=F32) / half)
    inv_row = _pad_cols(jnp.concatenate([inv, inv])[None, :], LANES)
    hd = MLA_NOPE_DIM + MLA_ROPE_DIM
    wq = w_uq.reshape(MLA_Q_RANK, MLA_HEADS, hd)
    zeros = jnp.zeros((MLA_Q_RANK, MLA_HEADS, LANES - MLA_ROPE_DIM), F32)
    wq_main = jnp.concatenate([wq, zeros], axis=2).reshape(MLA_Q_RANK, MLA_HEADS * 2 * LANES).astype(BF16)
    wq_swap = jnp.concatenate([-wq[:, :, MLA_NOPE_DIM + half:], wq[:, :, MLA_NOPE_DIM:MLA_NOPE_DIM + half], zeros],
                              axis=2).reshape(MLA_Q_RANK, MLA_HEADS * LANES).astype(BF16)
    qcat = _q_up(lat, q_g[None, :], posb, inv_row, wq_main, wq_swap, hd ** -0.5 * LOG2E, t.proj_m)
    kcat, vext = _kv_up(lat, kv_g[None, :], posb, inv_row, w_ukv.astype(BF16), t.kv_up_m)
    ob = _mla_attention(qcat, kcat, vext, batch, seq, t.attn_q)

    merged = _gated_merge(hb, oa, ob, wt_in[off_gate:off_gate + d], wt_in[off_gate + d:],
                          w_proj_a.astype(BF16), w_proj_b.astype(BF16), b_gate[None, :d], b_gate[None, d:],
                          t.merge_m, t.merge_n)
    y1 = _mm_res(merged, w_out.astype(BF16), h, alpha, t.proj_m, t.proj_n)

    tn = t.ffn_n
    nj = fp // tn

    def group(a):
        r = a.shape[0]
        gv = jnp.stack([_pad_cols(a[:, :f], fp).reshape(r, nj, tn), _pad_cols(a[:, f:], fp).reshape(r, nj, tn)], axis=2)
        return gv.reshape(r, nj * 2 * tn)

    act = _ffn_up(y1, ln1_g[None, :], ln1_b[None, :], _regroup_cast_w_up(w_up, f, tn, nj), group(conv_w),
                  group(conv_b[None, :]), seq, t.ffn_m, tn)
    w_down_p = _cast_pad_rows_bf16(w_down, fp, math.gcd(f, tn), "w_down_cast")
    return _ffn_down(act, w_down_p, y1, ln1_g[None, :], ln1_b[None, :], ln2_g[None, :], ln2_b[None, :], alpha,
                     t.ffn_m, t.ffn_k)


def kernel(x, positions, w_in, b_gate, da_lambda_q1, da_lambda_k1, da_lambda_q2, da_lambda_k2, da_subln_g, mla_q_norm_g, mla_kv_norm_g, w_uq, w_ukv, w_proj_a, w_proj_b, w_out, ln1_g, ln1_b, w_up, conv_w, conv_b, w_down, ln2_g, ln2_b):
    batch, seq, d = x.shape
    depth = w_in.shape[0]
    alpha = (2.0 * depth) ** 0.25
    m = batch * seq
    posf = positions.astype(F32).reshape(m)
    posb = jnp.broadcast_to(posf[:, None], (m, LANES))
    pos_rows = posf.reshape(batch, 1, seq)
    h = x.reshape(m, d)
    for l in range(depth):
        lambda_init = 0.8 - 0.6 * math.exp(-0.3 * l)
        lam_p = jnp.stack([da_lambda_q1[l], da_lambda_k1[l], da_lambda_q2[l], da_lambda_k2[l]]).astype(F32)
        h = _layer(h, posb, pos_rows, batch, seq, alpha, lambda_init, w_in[l], b_gate[l], lam_p, da_subln_g[l],
                   mla_q_norm_g[l], mla_kv_norm_g[l], w_uq[l], w_ukv[l], w_proj_a[l], w_proj_b[l], w_out[l],
                   ln1_g[l], ln1_b[l], w_up[l], conv_w[l], conv_b[l], w_down[l], ln2_g[l], ln2_b[l])
    return h.reshape(batch, seq, d)
```

```python
import functools
import math
from typing import NamedTuple

import numpy as np
import jax
import jax.numpy as jnp
from jax import lax
from jax.experimental import pallas as pl
from jax.experimental.pallas import tpu as pltpu

DA_HEADS = 8
DA_HEAD_DIM = 128
DA_V_DIM = 2 * DA_HEAD_DIM
MLA_HEADS = 16
MLA_Q_RANK = 1024
MLA_KV_RANK = 512
MLA_NOPE_DIM = 128
MLA_ROPE_DIM = 64
MLA_V_DIM = 128
ROPE_THETA = 10000.0
CONV_WIDTH = 3
LN_EPS = 1e-5
RMS_EPS = 1e-6

LANES = 128
SUBLANES = 8
VMEM_LIMIT = 56 * 1024 * 1024
NEG_BIG = -1e30
LOG2E = math.log2(math.e)
Q_UP_HEADS_PER_STEP = 8
FFN_COLS_PER_STEP = 512
FFN_DOWN_COL_CHUNK = 1024
FFN_DOWN_ROW_CHUNK = 64

F32 = jnp.float32
BF16 = jnp.bfloat16


def _params(sem):
    return pltpu.CompilerParams(dimension_semantics=sem, vmem_limit_bytes=VMEM_LIMIT)


def _tile(n, pref):
    t = min(n, pref)
    while n % t:
        t //= 2
    return t


def _dot(a, b):
    return jnp.dot(a, b, preferred_element_type=F32)


def _dot_nt(a, b):
    return lax.dot_general(a, b, (((1,), (1,)), ((), ())), preferred_element_type=F32)


def _mm_scale_kernel(a_ref, wt_ref, s_ref, o_ref):
    o_ref[...] = (_dot_nt(a_ref[...], wt_ref[...]) * s_ref[...]).astype(o_ref.dtype)


def _mm_scale(a, wt, scale, out_dtype, tm, tn, name):
    m, k = a.shape
    n = scale.shape[1]
    return pl.pallas_call(
        _mm_scale_kernel,
        grid=(m // tm, n // tn),
        in_specs=[pl.BlockSpec((tm, k), lambda i, j: (i, 0)),
                  pl.BlockSpec((tn, k), lambda i, j: (j, 0)),
                  pl.BlockSpec((1, tn), lambda i, j: (0, j))],
        out_specs=pl.BlockSpec((tm, tn), lambda i, j: (i, j)),
        out_shape=jax.ShapeDtypeStruct((m, n), out_dtype),
        compiler_params=_params(("parallel", "arbitrary")),
        name=name,
    )(a, wt, scale)


def _rms_bf16(c, g):
    ms = jnp.mean(c * c, axis=-1, keepdims=True)
    return (c * lax.rsqrt(ms + RMS_EPS) * g).astype(BF16)


def _qup_kernel(cq_ref, g_ref, pos_ref, inv_ref, wm_ref, ws_ref, o_ref, cn_ref, cos_ref, sin_ref, *, scale):
    @pl.when(pl.program_id(1) == 0)
    def _():
        cn_ref[...] = _rms_bf16(cq_ref[...], g_ref[...])
        ang = pos_ref[...] * inv_ref[...]
        cos_ref[...] = jnp.cos(ang)
        sin_ref[...] = jnp.sin(ang)

    cn = cn_ref[...]
    main = _dot(cn, wm_ref[...])
    swapped = _dot(cn, ws_ref[...])
    for hh in range(o_ref.shape[1] // (2 * LANES)):
        c0 = hh * 2 * LANES
        rope = main[:, c0 + LANES:c0 + 2 * LANES] * cos_ref[...] + swapped[:, hh * LANES:(hh + 1) * LANES] * sin_ref[...]
        o_ref[:, c0:c0 + LANES] = (main[:, c0:c0 + LANES] * scale).astype(o_ref.dtype)
        o_ref[:, c0 + LANES:c0 + 2 * LANES] = (rope * scale).astype(o_ref.dtype)


def _q_up(lat, g, posb, inv_row, w_main, w_swap, scale, tm):
    m = lat.shape[0]
    nh = MLA_HEADS
    hg = Q_UP_HEADS_PER_STEP
    return pl.pallas_call(
        functools.partial(_qup_kernel, scale=scale),
        grid=(m // tm, nh // hg),
        in_specs=[pl.BlockSpec((tm, MLA_Q_RANK), lambda i, h: (i, 0)),
                  pl.BlockSpec((1, MLA_Q_RANK), lambda i, h: (0, 0)),
                  pl.BlockSpec((tm, LANES), lambda i, h: (i, 0)),
                  pl.BlockSpec((1, LANES), lambda i, h: (0, 0)),
                  pl.BlockSpec((MLA_Q_RANK, hg * 2 * LANES), lambda i, h: (0, h)),
                  pl.BlockSpec((MLA_Q_RANK, hg * LANES), lambda i, h: (0, h))],
        out_specs=pl.BlockSpec((tm, hg * 2 * LANES), lambda i, h: (i, h)),
        out_shape=jax.ShapeDtypeStruct((m, nh * 2 * LANES), BF16),
        scratch_shapes=[pltpu.VMEM((tm, MLA_Q_RANK), BF16),
                        pltpu.VMEM((tm, LANES), F32),
                        pltpu.VMEM((tm, LANES), F32)],
        compiler_params=_params(("parallel", "arbitrary")),
        name="mla_q_up",
    )(lat, g, posb, inv_row, w_main, w_swap)


def _kvup_kernel(ckv_ref, kr_ref, krs_ref, g_ref, pos_ref, inv_ref, w_ref, k_ref, v_ref):
    tm = ckv_ref.shape[0]
    cn = _rms_bf16(ckv_ref[...], g_ref[...])
    ang = pos_ref[...] * inv_ref[...]
    krope = (kr_ref[...] * jnp.cos(ang) + krs_ref[...] * jnp.sin(ang)).astype(k_ref.dtype)
    ones_col = (lax.broadcasted_iota(jnp.int32, (tm, LANES), 1) == 0).astype(v_ref.dtype)
    wh = MLA_NOPE_DIM + MLA_V_DIM
    for h in range(MLA_HEADS):
        kv = _dot(cn, w_ref[:, h * wh:(h + 1) * wh])
        k_ref[:, h * wh:h * wh + MLA_NOPE_DIM] = kv[:, :MLA_NOPE_DIM].astype(k_ref.dtype)
        k_ref[:, h * wh + MLA_NOPE_DIM:(h + 1) * wh] = krope
        v_ref[:, h * wh:h * wh + MLA_V_DIM] = kv[:, MLA_NOPE_DIM:].astype(v_ref.dtype)
        v_ref[:, h * wh + MLA_V_DIM:(h + 1) * wh] = ones_col


def _kv_up(lat, g, posb, inv_row, w_ukv, tm):
    m = lat.shape[0]
    n = w_ukv.shape[1]
    ckv_blk = MLA_Q_RANK // MLA_KV_RANK
    kr_blk = (MLA_Q_RANK + MLA_KV_RANK) // LANES
    return pl.pallas_call(
        _kvup_kernel,
        grid=(m // tm,),
        in_specs=[pl.BlockSpec((tm, MLA_KV_RANK), lambda i: (i, ckv_blk)),
                  pl.BlockSpec((tm, LANES), lambda i: (i, kr_blk)),
                  pl.BlockSpec((tm, LANES), lambda i: (i, kr_blk + 1)),
                  pl.BlockSpec((1, MLA_KV_RANK), lambda i: (0, 0)),
                  pl.BlockSpec((tm, LANES), lambda i: (i, 0)),
                  pl.BlockSpec((1, LANES), lambda i: (0, 0)),
                  pl.BlockSpec((MLA_KV_RANK, n), lambda i: (0, 0))],
        out_specs=[pl.BlockSpec((tm, n), lambda i: (i, 0)),
                   pl.BlockSpec((tm, n), lambda i: (i, 0))],
        out_shape=[jax.ShapeDtypeStruct((m, n), BF16),
                   jax.ShapeDtypeStruct((m, n), BF16)],
        compiler_params=_params(("parallel",)),
        name="mla_kv_up",
    )(lat, lat, lat, g, posb, inv_row, w_ukv)


def _row_max(s):
    part = s[:, :LANES]
    for c in range(LANES, s.shape[1], LANES):
        part = jnp.maximum(part, s[:, c:c + LANES])
    return jnp.max(part, axis=1, keepdims=True)


def _lane_partial_sum(p):
    part = p[:, :LANES]
    for c in range(LANES, p.shape[1], LANES):
        part = part + p[:, c:c + LANES]
    return part


def _causal_mask(rows, cols, offset):
    row = lax.broadcasted_iota(jnp.int32, (rows, cols), 0)
    col = lax.broadcasted_iota(jnp.int32, (rows, cols), 1)
    return col <= row + offset


def _da_kernel(slope_ref, q_ref, k_ref, v_ref, pos_ref, lam_ref, g_ref, o_ref,
               m_ref, alpha_ref, l_ref, acc_ref, s_ref, p_ref, *, lambda_init):
    h = pl.program_id(1)
    qi = pl.program_id(2)
    d = DA_HEAD_DIM
    tq = q_ref.shape[0]
    tk = s_ref.shape[2]

    m_ref[...] = jnp.full(m_ref.shape, NEG_BIG, F32)
    l_ref[...] = jnp.zeros(l_ref.shape, F32)
    acc_ref[...] = jnp.zeros(acc_ref.shape, F32)

    slope2 = slope_ref[h] * LOG2E
    q0 = pl.multiple_of(qi * tq, tq)
    pq0 = pos_ref[:, pl.ds(q0, LANES)][:, :1]

    def scores(r, rows, start, width):
        cols = slice(r * d, (r + 1) * d)
        bias = slope2 * (pos_ref[:, pl.ds(start, width)] - pq0)
        return _dot_nt(q_ref[rows, cols], k_ref[pl.ds(start, width), cols]) + bias

    def diagonal():
        half = tq // 2
        for rows, width, offset in ((slice(0, half), half, 0), (slice(half, tq), tq, half)):
            v = v_ref[pl.ds(q0, width), :]
            for r in range(2):
                s = jnp.where(_causal_mask(half, width, offset), scores(r, rows, q0, width), NEG_BIG)
                m_old = m_ref[r, rows, :]
                m_new = jnp.maximum(m_old, _row_max(s))
                alpha = jnp.exp2(m_old - m_new)
                p = jnp.exp2(s - m_new)
                l_ref[r, rows, :] = alpha * l_ref[r, rows, :] + _lane_partial_sum(p)
                acc_ref[r, rows, :] = alpha * acc_ref[r, rows, :] + _dot(p.astype(v.dtype), v)
                m_ref[r, rows, :] = m_new

    def qk(r, c):
        s_ref[r] = scores(r, slice(None), pl.multiple_of(c * tk, tk), tk)

    def softmax(r):
        s = s_ref[r]
        m_old = m_ref[r]
        m_new = jnp.maximum(m_old, _row_max(s))
        alpha = jnp.exp2(m_old - m_new)
        p = jnp.exp2(s - m_new)
        alpha_ref[r] = alpha
        l_ref[r] = alpha * l_ref[r] + _lane_partial_sum(p)
        p_ref[r] = p.astype(p_ref.dtype)
        m_ref[r] = m_new

    def pv(r, c):
        v = v_ref[pl.ds(pl.multiple_of(c * tk, tk), tk), :]
        acc_ref[r] = alpha_ref[r] * acc_ref[r] + _dot(p_ref[r], v)

    _two_stream_pipeline(qi, diagonal, qk, softmax, pv)

    lam_p = lam_ref[...]
    lam = (jnp.exp(jnp.sum(lam_p[0:1] * lam_p[1:2], axis=1, keepdims=True))
           - jnp.exp(jnp.sum(lam_p[2:3] * lam_p[3:4], axis=1, keepdims=True)) + lambda_init)
    l0 = jnp.sum(l_ref[0], axis=1, keepdims=True)
    l1 = jnp.sum(l_ref[1], axis=1, keepdims=True)
    o = acc_ref[0] / l0 - lam * (acc_ref[1] / l1)
    ms = jnp.mean(o * o, axis=-1, keepdims=True)
    o_ref[...] = (o * lax.rsqrt(ms + RMS_EPS) * g_ref[...] * (1.0 - lambda_init)).astype(o_ref.dtype)


def _diff_attention(qkv, pos_rows, lam_p, g, slopes, batch, seq, tq, lambda_init):
    m = qkv.shape[0]
    nh = DA_HEADS
    nq = seq // tq
    w = DA_V_DIM
    grid_spec = pltpu.PrefetchScalarGridSpec(
        num_scalar_prefetch=1,
        grid=(batch, nh, nq),
        in_specs=[pl.BlockSpec((tq, w), lambda b, h, qi, sl: (b * nq + qi, h)),
                  pl.BlockSpec((seq, w), lambda b, h, qi, sl: (b, nh + h)),
                  pl.BlockSpec((seq, w), lambda b, h, qi, sl: (b, 2 * nh + h)),
                  pl.BlockSpec((None, 1, seq), lambda b, h, qi, sl: (b, 0, 0)),
                  pl.BlockSpec((4, DA_HEAD_DIM), lambda b, h, qi, sl: (0, 0)),
                  pl.BlockSpec((1, w), lambda b, h, qi, sl: (0, 0))],
        out_specs=pl.BlockSpec((tq, w), lambda b, h, qi, sl: (b * nq + qi, h)),
        scratch_shapes=[pltpu.VMEM((2, tq, 1), F32),
                        pltpu.VMEM((2, tq, 1), F32),
                        pltpu.VMEM((2, tq, LANES), F32),
                        pltpu.VMEM((2, tq, w), F32),
                        pltpu.VMEM((2, tq, tq), F32),
                        pltpu.VMEM((2, tq, tq), BF16)],
    )
    return pl.pallas_call(
        functools.partial(_da_kernel, lambda_init=lambda_init),
        grid_spec=grid_spec,
        out_shape=jax.ShapeDtypeStruct((m, nh * w), BF16),
        compiler_params=_params(("parallel", "parallel", "arbitrary")),
        name="diff_attention",
    )(slopes, qkv, qkv, qkv, pos_rows, lam_p, g)


def _two_stream_pipeline(n, head, qk, softmax, pv):
    @pl.when(n == 0)
    def _():
        head()

    @pl.when(n > 0)
    def _():
        head()
        qk(0, 0)
        softmax(0)
        qk(1, 0)

        def trip(c, carry):
            pv(0, c - 1)
            softmax(1)
            qk(0, c)
            pv(1, c - 1)
            softmax(0)
            qk(1, c)
            return carry

        lax.fori_loop(1, n, trip, 0)
        pv(0, n - 1)
        softmax(1)
        pv(1, n - 1)


def _mla_kernel(q_ref, k_ref, v_ref, o_ref, m_ref, alpha_ref, acc_ref, s_ref, p_ref):
    qi = pl.program_id(2)
    tq = q_ref.shape[0]
    half = tq // 2

    m_ref[...] = jnp.full(m_ref.shape, NEG_BIG, F32)
    acc_ref[...] = jnp.zeros(acc_ref.shape, F32)

    def rows(r):
        return slice(r * half, (r + 1) * half)

    def diagonal(r, start, width, offset):
        s = _dot_nt(q_ref[rows(r), :], k_ref[pl.ds(start, width), :])
        s = jnp.where(_causal_mask(half, width, offset), s, NEG_BIG)
        m_old = m_ref[r]
        m_new = jnp.maximum(m_old, _row_max(s))
        p = jnp.exp2(s - m_new).astype(v_ref.dtype)
        acc_ref[r] = jnp.exp2(m_old - m_new) * acc_ref[r] + _dot(p, v_ref[pl.ds(start, width), :])
        m_ref[r] = m_new

    def qk(r, c):
        s_ref[r] = _dot_nt(q_ref[rows(r), :], k_ref[pl.ds(pl.multiple_of(c * tq, tq), tq), :])

    def softmax(r):
        s = s_ref[r]
        m_old = m_ref[r]
        m_new = jnp.maximum(m_old, _row_max(s))
        alpha_ref[r] = jnp.exp2(m_old - m_new)
        p_ref[r] = jnp.exp2(s - m_new).astype(p_ref.dtype)
        m_ref[r] = m_new

    def pv(r, c):
        v = v_ref[pl.ds(pl.multiple_of(c * tq, tq), tq), :]
        acc_ref[r] = alpha_ref[r] * acc_ref[r] + _dot(p_ref[r], v)

    d0 = pl.multiple_of(qi * tq, tq)

    def head():
        diagonal(0, d0, half, 0)
        diagonal(1, d0, tq, half)

    _two_stream_pipeline(qi, head, qk, softmax, pv)
    for r in range(2):
        acc = acc_ref[r]
        o_ref[rows(r), :] = (acc[:, :MLA_V_DIM] / acc[:, MLA_V_DIM:MLA_V_DIM + 1]).astype(o_ref.dtype)


def _mla_attention(qcat, kcat, vext, batch, seq, tq):
    m = qcat.shape[0]
    nh = MLA_HEADS
    nq = seq // tq
    w = 2 * LANES
    return pl.pallas_call(
        _mla_kernel,
        grid=(batch, nh, nq),
        in_specs=[pl.BlockSpec((tq, w), lambda b, h, qi: (b * nq + qi, h)),
                  pl.BlockSpec((seq, w), lambda b, h, qi: (b, h)),
                  pl.BlockSpec((seq, w), lambda b, h, qi: (b, h))],
        out_specs=pl.BlockSpec((tq, MLA_V_DIM), lambda b, h, qi: (b * nq + qi, h)),
        out_shape=jax.ShapeDtypeStruct((m, nh * MLA_V_DIM), BF16),
        scratch_shapes=[pltpu.VMEM((2, tq // 2, 1), F32),
                        pltpu.VMEM((2, tq // 2, 1), F32),
                        pltpu.VMEM((2, tq // 2, w), F32),
                        pltpu.VMEM((2, tq // 2, tq), F32),
                        pltpu.VMEM((2, tq // 2, tq), BF16)],
        compiler_params=_params(("parallel", "parallel", "arbitrary")),
        name="mla_attention",
    )(qcat, kcat, vext)


def _merge_kernel(x_ref, oa_ref, ob_ref, wga_ref, wgb_ref, wpa_ref, wpb_ref, ba_ref, bb_ref, o_ref):
    d = x_ref.shape[1]
    kc = oa_ref.shape[1] if d % oa_ref.shape[1] == 0 else d

    def gate(wt_ref, b_ref):
        acc = b_ref[...] + _dot_nt(x_ref[:, :kc], wt_ref[:, :kc])
        for c in range(kc, d, kc):
            acc = acc + _dot_nt(x_ref[:, c:c + kc], wt_ref[:, c:c + kc])
        return jax.nn.sigmoid(acc)

    ga = gate(wga_ref, ba_ref)
    gb = gate(wgb_ref, bb_ref)
    pa = _dot(oa_ref[...], wpa_ref[...])
    pb = _dot(ob_ref[...], wpb_ref[...])
    o_ref[...] = (ga * pa + gb * pb).astype(o_ref.dtype)


def _gated_merge(xb, oa, ob, wga_t, wgb_t, wpa, wpb, ba, bb, tm, tn):
    m, d = xb.shape
    n = wpa.shape[1]
    ko = oa.shape[1]
    return pl.pallas_call(
        _merge_kernel,
        grid=(m // tm, n // tn),
        in_specs=[pl.BlockSpec((tm, d), lambda i, j: (i, 0)),
                  pl.BlockSpec((tm, ko), lambda i, j: (i, 0)),
                  pl.BlockSpec((tm, ko), lambda i, j: (i, 0)),
                  pl.BlockSpec((tn, d), lambda i, j: (j, 0)),
                  pl.BlockSpec((tn, d), lambda i, j: (j, 0)),
                  pl.BlockSpec((ko, tn), lambda i, j: (0, j)),
                  pl.BlockSpec((ko, tn), lambda i, j: (0, j)),
                  pl.BlockSpec((1, tn), lambda i, j: (0, j)),
                  pl.BlockSpec((1, tn), lambda i, j: (0, j))],
        out_specs=pl.BlockSpec((tm, tn), lambda i, j: (i, j)),
        out_shape=jax.ShapeDtypeStruct((m, n), BF16),
        compiler_params=_params(("parallel", "arbitrary")),
        name="gated_merge",
    )(xb, oa, ob, wga_t, wgb_t, wpa, wpb, ba, bb)


def _mm_res_kernel(a_ref, w_ref, r_ref, o_ref, *, alpha):
    o_ref[...] = alpha * r_ref[...] + _dot(a_ref[...], w_ref[...])


def _mm_res(a, w, res, alpha, tm, tn):
    m, k = a.shape
    n = w.shape[1]
    return pl.pallas_call(
        functools.partial(_mm_res_kernel, alpha=alpha),
        grid=(m // tm, n // tn),
        in_specs=[pl.BlockSpec((tm, k), lambda i, j: (i, 0)),
                  pl.BlockSpec((k, tn), lambda i, j: (0, j)),
                  pl.BlockSpec((tm, tn), lambda i, j: (i, j))],
        out_specs=pl.BlockSpec((tm, tn), lambda i, j: (i, j)),
        out_shape=jax.ShapeDtypeStruct((m, n), F32),
        compiler_params=_params(("parallel", "arbitrary")),
        name="out_proj_residual",
    )(a, w, res)


def _layernorm(y, g, b):
    mu = jnp.mean(y, axis=-1, keepdims=True)
    yc = y - mu
    var = jnp.mean(yc * yc, axis=-1, keepdims=True)
    return yc * lax.rsqrt(var + LN_EPS) * g + b


def _ffn_up_kernel(y_ref, g_ref, b_ref, w_ref, cw_ref, cb_ref, o_ref, h_ref, u_ref, carry_ref, *, nj, tiles_per_seq):
    t = pl.program_id(0)
    last = pl.num_programs(0) - 1
    tm = y_ref.shape[0]
    half = o_ref.shape[1]
    pad = SUBLANES

    @pl.when(t == 0)
    def _():
        u_ref[...] = jnp.zeros(u_ref.shape, F32)
        carry_ref[...] = jnp.zeros(carry_ref.shape, F32)

    @pl.when((t % nj == 0) & (t < last))
    def _():
        h_ref[...] = _layernorm(y_ref[...], g_ref[...], b_ref[...]).astype(h_ref.dtype)

    tp = jnp.maximum(t - 1, 0)
    ip = tp // nj
    jp = tp % nj
    seq_start = (ip % tiles_per_seq == 0).astype(F32)
    u_ref[:pad, :] = carry_ref[jp] * (1.0 - seq_start)
    cw = cw_ref[...]
    c = cb_ref[...] + cw[2:3] * u_ref[pad:, :]
    for back in range(1, CONV_WIDTH):
        c = c + cw[CONV_WIDTH - 1 - back:CONV_WIDTH - back] * u_ref[pad - back:pad - back + tm, :]
    carry_ref[jp] = u_ref[tm:, :]
    gate = c[:, :half]
    o_ref[...] = (gate * jax.nn.sigmoid(gate) * c[:, half:]).astype(o_ref.dtype)

    u_ref[pad:, :] = _dot(h_ref[...], w_ref[...])


def _ffn_up(y, g, b, w_up, conv_w, conv_b, seq, tm, tn):
    m, d = y.shape
    n2 = w_up.shape[1]
    nj = n2 // (2 * tn)
    steps = (m // tm) * nj

    def cur(t):
        return jnp.minimum(t, steps - 1)

    def prev(t):
        return jnp.maximum(t - 1, 0)

    return pl.pallas_call(
        functools.partial(_ffn_up_kernel, nj=nj, tiles_per_seq=seq // tm),
        grid=(steps + 1,),
        in_specs=[pl.BlockSpec((tm, d), lambda t: (cur(t) // nj, 0)),
                  pl.BlockSpec((1, d), lambda t: (0, 0)),
                  pl.BlockSpec((1, d), lambda t: (0, 0)),
                  pl.BlockSpec((d, 2 * tn), lambda t: (0, cur(t) % nj)),
                  pl.BlockSpec((CONV_WIDTH, 2 * tn), lambda t: (0, prev(t) % nj)),
                  pl.BlockSpec((1, 2 * tn), lambda t: (0, prev(t) % nj))],
        out_specs=pl.BlockSpec((tm, tn), lambda t: (prev(t) // nj, prev(t) % nj)),
        out_shape=jax.ShapeDtypeStruct((m, n2 // 2), BF16),
        scratch_shapes=[pltpu.VMEM((tm, d), BF16),
                        pltpu.VMEM((tm + SUBLANES, 2 * tn), F32),
                        pltpu.VMEM((nj, SUBLANES, 2 * tn), F32)],
        compiler_params=_params(("arbitrary",)),
        name="ffn_up_conv_gate",
    )(y, g, b, w_up, conv_w, conv_b)


def _ffn_down_kernel(a_ref, w_ref, y_ref, g1_ref, b1_ref, g2_ref, b2_ref, o_ref, *, alpha):
    k = pl.program_id(1)
    tm, d = o_ref.shape

    @pl.when(k == 0)
    def _():
        o_ref[...] = jnp.zeros(o_ref.shape, F32)

    a = a_ref[...]
    for c in range(0, d, FFN_DOWN_COL_CHUNK):
        cols = slice(c, min(c + FFN_DOWN_COL_CHUNK, d))
        o_ref[:, cols] += _dot(a, w_ref[:, cols])

    @pl.when(k == pl.num_programs(1) - 1)
    def _():
        rows_per = min(tm, FFN_DOWN_ROW_CHUNK)

        def body(r, carry):
            rows = pl.ds(pl.multiple_of(r * rows_per, rows_per), rows_per)
            h = _layernorm(y_ref[rows, :], g1_ref[...], b1_ref[...])
            o_ref[rows, :] = _layernorm(alpha * h + o_ref[rows, :], g2_ref[...], b2_ref[...])
            return carry

        lax.fori_loop(0, tm // rows_per, body, 0)


def _ffn_down(act, w_down, y, g1, b1, g2, b2, alpha, tm, tk):
    m, kdim = act.shape
    d = w_down.shape[1]
    vec = pl.BlockSpec((1, d), lambda i, k: (0, 0))
    return pl.pallas_call(
        functools.partial(_ffn_down_kernel, alpha=alpha),
        grid=(m // tm, kdim // tk),
        in_specs=[pl.BlockSpec((tm, tk), lambda i, k: (i, k)),
                  pl.BlockSpec((tk, d), lambda i, k: (k, 0)),
                  pl.BlockSpec((tm, d), lambda i, k: (i, 0)),
                  vec, vec, vec, vec],
        out_specs=pl.BlockSpec((tm, d), lambda i, k: (i, 0)),
        out_shape=jax.ShapeDtypeStruct((m, d), F32),
        compiler_params=_params(("parallel", "arbitrary")),
        name="ffn_down_layernorm",
    )(act, w_down, y, g1, b1, g2, b2)


def _cast_kernel(x_ref, o_ref):
    o_ref[...] = x_ref[...].astype(o_ref.dtype)


def _cast_bf16(w, rb, cb, name):
    rows, cols = w.shape
    return pl.pallas_call(
        _cast_kernel,
        grid=(pl.cdiv(rows, rb), pl.cdiv(cols, cb)),
        in_specs=[pl.BlockSpec((rb, cb), lambda i, j: (i, j))],
        out_specs=pl.BlockSpec((rb, cb), lambda i, j: (i, j)),
        out_shape=jax.ShapeDtypeStruct((rows, cols), BF16),
        compiler_params=_params(("parallel", "parallel")),
        name=name,
    )(w)


def _cast_pad_kernel(x_ref, o_ref, *, n_src):
    keep = pl.program_id(0) < n_src
    o_ref[...] = jnp.where(keep, x_ref[...], 0.0).astype(o_ref.dtype)


def _cast_pad_rows_bf16(w, rows_out, rb, name):
    rows, cols = w.shape
    n_src = rows // rb
    return pl.pallas_call(
        functools.partial(_cast_pad_kernel, n_src=n_src),
        grid=(rows_out // rb,),
        in_specs=[pl.BlockSpec((rb, cols), lambda i: (jnp.minimum(i, n_src - 1), 0))],
        out_specs=pl.BlockSpec((rb, cols), lambda i: (i, 0)),
        out_shape=jax.ShapeDtypeStruct((rows_out, cols), BF16),
        compiler_params=_params(("parallel",)),
        name=name,
    )(w)


def _regroup_kernel(src_ref, keep_ref, x_ref, o_ref):
    keep = keep_ref[pl.program_id(0)] > 0
    o_ref[...] = jnp.where(keep, x_ref[...], 0.0).astype(o_ref.dtype)


def _regroup_cast_w_up(w_up, f, tn, nj):
    d = w_up.shape[0]
    cb = math.gcd(f, tn)
    sub = tn // cb
    nsrc = f // cb
    src, keep = [], []
    for j in range(nj):
        for base in (0, nsrc):
            for r in range(sub):
                g = j * sub + r
                src.append(base + min(g, nsrc - 1))
                keep.append(1 if g < nsrc else 0)
    grid_spec = pltpu.PrefetchScalarGridSpec(
        num_scalar_prefetch=2,
        grid=(len(src),),
        in_specs=[pl.BlockSpec((d, cb), lambda b, src, keep: (0, src[b]))],
        out_specs=pl.BlockSpec((d, cb), lambda b, src, keep: (0, b)),
    )
    return pl.pallas_call(
        _regroup_kernel,
        grid_spec=grid_spec,
        out_shape=jax.ShapeDtypeStruct((d, nj * 2 * tn), BF16),
        compiler_params=_params(("parallel",)),
        name="w_up_regroup_cast",
    )(jnp.asarray(src, jnp.int32), jnp.asarray(keep, jnp.int32), w_up)


def _rope_rows(wt):
    half = MLA_ROPE_DIM // 2
    return jnp.concatenate([-wt[half:], wt[:half]], axis=0)


def _pad_rows(w, n):
    return jnp.pad(w, ((0, n - w.shape[0]), (0, 0)))


def _pad_cols(w, n):
    return jnp.pad(w, ((0, 0), (0, n - w.shape[1])))


class _Tiles(NamedTuple):
    proj_m: int
    proj_n: int
    kv_up_m: int
    attn_q: int
    merge_m: int
    merge_n: int
    ffn_m: int
    ffn_n: int
    ffn_k: int
    cast: int


def _plan_tiles(m, d, seq, fp):
    return _Tiles(proj_m=_tile(m, 1024), proj_n=_tile(d, 1024), kv_up_m=_tile(m, 512), attn_q=_tile(seq, 1024),
                  merge_m=_tile(m, 512), merge_n=_tile(d, 512), ffn_m=_tile(seq, 512), ffn_n=FFN_COLS_PER_STEP,
                  ffn_k=_tile(fp, 1024), cast=1024)


def _layer(h, posb, pos_rows, batch, seq, alpha, lambda_init, w_in, b_gate, lam_p, da_g, q_g, kv_g, w_uq, w_ukv,
           w_proj_a, w_proj_b, w_out, ln1_g, ln1_b, w_up, conv_w, conv_b, w_down, ln2_g, ln2_b):
    m, d = h.shape
    f = w_down.shape[0]
    fp = -(-f // FFN_COLS_PER_STEP) * FFN_COLS_PER_STEP
    t = _plan_tiles(m, d, seq, fp)
    da_cols = 3 * DA_HEADS * DA_V_DIM
    off_cq = da_cols
    off_ckv = off_cq + MLA_Q_RANK
    off_kr = off_ckv + MLA_KV_RANK
    off_gate = off_kr + MLA_ROPE_DIM

    hb = h.astype(BF16)

    da_scale = DA_HEAD_DIM ** -0.5 * LOG2E
    qkv_scale = jnp.concatenate([jnp.full((DA_HEADS * DA_V_DIM,), da_scale, F32),
                                 jnp.ones((2 * DA_HEADS * DA_V_DIM,), F32)])[None, :]
    wt_in = _cast_bf16(jnp.swapaxes(w_in, 0, 1), t.cast, _tile(d, t.cast), "w_in_cast")
    qkv = _mm_scale(hb, wt_in, qkv_scale, BF16, t.proj_m, t.proj_n, "da_qkv_proj")

    wt_kr = wt_in[off_kr:off_gate]
    wt_lat = jnp.concatenate([wt_in[off_cq:off_kr], _pad_rows(wt_kr, LANES), _pad_rows(_rope_rows(wt_kr), LANES)], axis=0)
    lat = _mm_scale(hb, wt_lat, jnp.ones((1, wt_lat.shape[0]), F32), F32, t.proj_m, wt_lat.shape[0] // 2,
                    "mla_latent_proj")

    slopes = 2.0 ** (-8.0 * jnp.arange(1, DA_HEADS + 1, dtype=F32) / DA_HEADS)
    oa = _diff_attention(qkv, pos_rows, lam_p, da_g[None, :], slopes, batch, seq, t.attn_q, lambda_init)

    half = MLA_ROPE_DIM // 2
    inv = ROPE_THETA ** (-jnp.arange(half, dtype=F32) / half)
    inv_row = _pad_cols(jnp.concatenate([inv, inv])[None, :], LANES)
    hd = MLA_NOPE_DIM + MLA_ROPE_DIM
    wq = w_uq.reshape(MLA_Q_RANK, MLA_HEADS, hd)
    zeros = jnp.zeros((MLA_Q_RANK, MLA_HEADS, LANES - MLA_ROPE_DIM), F32)
    wq_main = jnp.concatenate([wq, zeros], axis=2).reshape(MLA_Q_RANK, MLA_HEADS * 2 * LANES).astype(BF16)
    wq_swap = jnp.concatenate([-wq[:, :, MLA_NOPE_DIM + half:], wq[:, :, MLA_NOPE_DIM:MLA_NOPE_DIM + half], zeros],
                              axis=2).reshape(MLA_Q_RANK, MLA_HEADS * LANES).astype(BF16)
    qcat = _q_up(lat, q_g[None, :], posb, inv_row, wq_main, wq_swap, hd ** -0.5 * LOG2E, t.proj_m)
    kcat, vext = _kv_up(lat, kv_g[None, :], posb, inv_row, w_ukv.astype(BF16), t.kv_up_m)
    ob = _mla_attention(qcat, kcat, vext, batch, seq, t.attn_q)

    merged = _gated_merge(hb, oa, ob, wt_in[off_gate:off_gate + d], wt_in[off_gate + d:],
                          w_proj_a.astype(BF16), w_proj_b.astype(BF16), b_gate[None, :d], b_gate[None, d:],
                          t.merge_m, t.merge_n)
    y1 = _mm_res(merged, w_out.astype(BF16), h, alpha, t.proj_m, t.proj_n)

    tn = t.ffn_n
    nj = fp // tn

    def group(a):
        r = a.shape[0]
        gv = jnp.stack([_pad_cols(a[:, :f], fp).reshape(r, nj, tn), _pad_cols(a[:, f:], fp).reshape(r, nj, tn)], axis=2)
        return gv.reshape(r, nj * 2 * tn)

    act = _ffn_up(y1, ln1_g[None, :], ln1_b[None, :], _regroup_cast_w_up(w_up, f, tn, nj), group(conv_w),
                  group(conv_b[None, :]), seq, t.ffn_m, tn)
    w_down_p = _cast_pad_rows_bf16(w_down, fp, math.gcd(f, tn), "w_down_cast")
    return _ffn_down(act, w_down_p, y1, ln1_g[None, :], ln1_b[None, :], ln2_g[None, :], ln2_b[None, :], alpha,
                     t.ffn_m, t.ffn_k)


def kernel(x, positions, w_in, b_gate, da_lambda_q1, da_lambda_k1, da_lambda_q2, da_lambda_k2, da_subln_g, mla_q_norm_g, mla_kv_norm_g, w_uq, w_ukv, w_proj_a, w_proj_b, w_out, ln1_g, ln1_b, w_up, conv_w, conv_b, w_down, ln2_g, ln2_b):
    batch, seq, d = x.shape
    depth = w_in.shape[0]
    alpha = (2.0 * depth) ** 0.25
    m = batch * seq
    posf = positions.astype(F32).reshape(m)
    posb = jnp.broadcast_to(posf[:, None], (m, LANES))
    pos_rows = posf.reshape(batch, 1, seq)
    h = x.reshape(m, d)
    for l in range(depth):
        lambda_init = 0.8 - 0.6 * math.exp(-0.3 * l)
        lam_p = jnp.stack([da_lambda_q1[l], da_lambda_k1[l], da_lambda_q2[l], da_lambda_k2[l]]).astype(F32)
        h = _layer(h, posb, pos_rows, batch, seq, alpha, lambda_init, w_in[l], b_gate[l], lam_p, da_subln_g[l],
                   mla_q_norm_g[l], mla_kv_norm_g[l], w_uq[l], w_ukv[l], w_proj_a[l], w_proj_b[l], w_out[l],
                   ln1_g[l], ln1_b[l], w_up[l], conv_w[l], conv_b[l], w_down[l], ln2_g[l], ln2_b[l])
    return h.reshape(batch, seq, d)
```

```python
import functools
import math
from typing import NamedTuple

import numpy as np
import jax
import jax.numpy as jnp
from jax import lax
from jax.experimental import pallas as pl
from jax.experimental.pallas import tpu as pltpu

DA_HEADS = 8
DA_HEAD_DIM = 128
DA_V_DIM = 2 * DA_HEAD_DIM
MLA_HEADS = 16
MLA_Q_RANK = 1024
MLA_KV_RANK = 512
MLA_NOPE_DIM = 128
MLA_ROPE_DIM = 64
MLA_V_DIM = 128
ROPE_THETA = 10000.0
CONV_WIDTH = 3
LN_EPS = 1e-5
RMS_EPS = 1e-6

LANES = 128
SUBLANES = 8
VMEM_LIMIT = 56 * 1024 * 1024
NEG_BIG = -1e30
LOG2E = math.log2(math.e)
Q_UP_HEADS_PER_STEP = 8
FFN_COLS_PER_STEP = 512
FFN_DOWN_COL_CHUNK = 1024
FFN_DOWN_ROW_CHUNK = 64

F32 = jnp.float32
BF16 = jnp.bfloat16


def _params(sem):
    return pltpu.CompilerParams(dimension_semantics=sem, vmem_limit_bytes=VMEM_LIMIT)


def _tile(n, pref):
    t = min(n, pref)
    while n % t:
        t //= 2
    return t


def _dot(a, b):
    return jnp.dot(a, b, preferred_element_type=F32)


def _dot_nt(a, b):
    return lax.dot_general(a, b, (((1,), (1,)), ((), ())), preferred_element_type=F32)


def _mm_scale_kernel(a_ref, wt_ref, s_ref, o_ref):
    o_ref[...] = (_dot_nt(a_ref[...], wt_ref[...]) * s_ref[...]).astype(o_ref.dtype)


def _mm_scale(a, wt, scale, out_dtype, tm, tn, name):
    m, k = a.shape
    n = scale.shape[1]
    return pl.pallas_call(
        _mm_scale_kernel,
        grid=(m // tm, n // tn),
        in_specs=[pl.BlockSpec((tm, k), lambda i, j: (i, 0)),
                  pl.BlockSpec((tn, k), lambda i, j: (j, 0)),
                  pl.BlockSpec((1, tn), lambda i, j: (0, j))],
        out_specs=pl.BlockSpec((tm, tn), lambda i, j: (i, j)),
        out_shape=jax.ShapeDtypeStruct((m, n), out_dtype),
        compiler_params=_params(("parallel", "arbitrary")),
        name=name,
    )(a, wt, scale)


def _rms_bf16(c, g):
    ms = jnp.mean(c * c, axis=-1, keepdims=True)
    return (c * lax.rsqrt(ms + RMS_EPS) * g).astype(BF16)


def _qup_kernel(cq_ref, g_ref, pos_ref, inv_ref, wm_ref, ws_ref, o_ref, cn_ref, cos_ref, sin_ref, *, scale):
    @pl.when(pl.program_id(1) == 0)
    def _():
        cn_ref[...] = _rms_bf16(cq_ref[...], g_ref[...])
        ang = pos_ref[...] * inv_ref[...]
        cos_ref[...] = jnp.cos(ang)
        sin_ref[...] = jnp.sin(ang)

    cn = cn_ref[...]
    main = _dot(cn, wm_ref[...])
    swapped = _dot(cn, ws_ref[...])
    for hh in range(o_ref.shape[1] // (2 * LANES)):
        c0 = hh * 2 * LANES
        rope = main[:, c0 + LANES:c0 + 2 * LANES] * cos_ref[...] + swapped[:, hh * LANES:(hh + 1) * LANES] * sin_ref[...]
        o_ref[:, c0:c0 + LANES] = (main[:, c0:c0 + LANES] * scale).astype(o_ref.dtype)
        o_ref[:, c0 + LANES:c0 + 2 * LANES] = (rope * scale).astype(o_ref.dtype)


def _q_up(lat, g, posb, inv_row, w_main, w_swap, scale, tm):
    m = lat.shape[0]
    nh = MLA_HEADS
    hg = Q_UP_HEADS_PER_STEP
    return pl.pallas_call(
        functools.partial(_qup_kernel, scale=scale),
        grid=(m // tm, nh // hg),
        in_specs=[pl.BlockSpec((tm, MLA_Q_RANK), lambda i, h: (i, 0)),
                  pl.BlockSpec((1, MLA_Q_RANK), lambda i, h: (0, 0)),
                  pl.BlockSpec((tm, LANES), lambda i, h: (i, 0)),
                  pl.BlockSpec((1, LANES), lambda i, h: (0, 0)),
                  pl.BlockSpec((MLA_Q_RANK, hg * 2 * LANES), lambda i, h: (0, h)),
                  pl.BlockSpec((MLA_Q_RANK, hg * LANES), lambda i, h: (0, h))],
        out_specs=pl.BlockSpec((tm, hg * 2 * LANES), lambda i, h: (i, h)),
        out_shape=jax.ShapeDtypeStruct((m, nh * 2 * LANES), BF16),
        scratch_shapes=[pltpu.VMEM((tm, MLA_Q_RANK), BF16),
                        pltpu.VMEM((tm, LANES), F32),
                        pltpu.VMEM((tm, LANES), F32)],
        compiler_params=_params(("parallel", "arbitrary")),
        name="mla_q_up",
    )(lat, g, posb, inv_row, w_main, w_swap)


def _kvup_kernel(ckv_ref, kr_ref, krs_ref, g_ref, pos_ref, inv_ref, w_ref, k_ref, v_ref):
    tm = ckv_ref.shape[0]
    cn = _rms_bf16(ckv_ref[...], g_ref[...])
    ang = pos_ref[...] * inv_ref[...]
    krope = (kr_ref[...] * jnp.cos(ang) + krs_ref[...] * jnp.sin(ang)).astype(k_ref.dtype)
    ones_col = (lax.broadcasted_iota(jnp.int32, (tm, LANES), 1) == 0).astype(v_ref.dtype)
    wh = MLA_NOPE_DIM + MLA_V_DIM
    for h in range(MLA_HEADS):
        kv = _dot(cn, w_ref[:, h * wh:(h + 1) * wh])
        k_ref[:, h * wh:h * wh + MLA_NOPE_DIM] = kv[:, :MLA_NOPE_DIM].astype(k_ref.dtype)
        k_ref[:, h * wh + MLA_NOPE_DIM:(h + 1) * wh] = krope
        v_ref[:, h * wh:h * wh + MLA_V_DIM] = kv[:, MLA_NOPE_DIM:].astype(v_ref.dtype)
        v_ref[:, h * wh + MLA_V_DIM:(h + 1) * wh] = ones_col


def _kv_up(lat, g, posb, inv_row, w_ukv, tm):
    m = lat.shape[0]
    n = w_ukv.shape[1]
    ckv_blk = MLA_Q_RANK // MLA_KV_RANK
    kr_blk = (MLA_Q_RANK + MLA_KV_RANK) // LANES
    return pl.pallas_call(
        _kvup_kernel,
        grid=(m // tm,),
        in_specs=[pl.BlockSpec((tm, MLA_KV_RANK), lambda i: (i, ckv_blk)),
                  pl.BlockSpec((tm, LANES), lambda i: (i, kr_blk)),
                  pl.BlockSpec((tm, LANES), lambda i: (i, kr_blk + 1)),
                  pl.BlockSpec((1, MLA_KV_RANK), lambda i: (0, 0)),
                  pl.BlockSpec((tm, LANES), lambda i: (i, 0)),
                  pl.BlockSpec((1, LANES), lambda i: (0, 0)),
                  pl.BlockSpec((MLA_KV_RANK, n), lambda i: (0, 0))],
        out_specs=[pl.BlockSpec((tm, n), lambda i: (i, 0)),
                   pl.BlockSpec((tm, n), lambda i: (i, 0))],
        out_shape=[jax.ShapeDtypeStruct((m, n), BF16),
                   jax.ShapeDtypeStruct((m, n), BF16)],
        compiler_params=_params(("parallel",)),
        name="mla_kv_up",
    )(lat, lat, lat, g, posb, inv_row, w_ukv)


def _row_max(s):
    part = s[:, :LANES]
    for c in range(LANES, s.shape[1], LANES):
        part = jnp.maximum(part, s[:, c:c + LANES])
    return jnp.max(part, axis=1, keepdims=True)


def _lane_partial_sum(p):
    part = p[:, :LANES]
    for c in range(LANES, p.shape[1], LANES):
        part = part + p[:, c:c + LANES]
    return part


def _causal_mask(rows, cols, offset):
    row = lax.broadcasted_iota(jnp.int32, (rows, cols), 0)
    col = lax.broadcasted_iota(jnp.int32, (rows, cols), 1)
    return col <= row + offset


def _da_kernel(slope_ref, q_ref, k_ref, v_ref, pos_ref, lam_ref, g_ref, o_ref,
               m_ref, alpha_ref, l_ref, acc_ref, s_ref, p_ref, *, lambda_init):
    h = pl.program_id(1)
    qi = pl.program_id(2)
    d = DA_HEAD_DIM
    tq = q_ref.shape[0]
    tk = s_ref.shape[2]

    m_ref[...] = jnp.full(m_ref.shape, NEG_BIG, F32)
    l_ref[...] = jnp.zeros(l_ref.shape, F32)
    acc_ref[...] = jnp.zeros(acc_ref.shape, F32)

    slope2 = slope_ref[h] * LOG2E
    q0 = pl.multiple_of(qi * tq, tq)
    pq0 = pos_ref[:, pl.ds(q0, LANES)][:, :1]

    def scores(r, rows, start, width):
        cols = slice(r * d, (r + 1) * d)
        bias = slope2 * (pos_ref[:, pl.ds(start, width)] - pq0)
        return _dot_nt(q_ref[rows, cols], k_ref[pl.ds(start, width), cols]) + bias

    def diagonal():
        half = tq // 2
        for rows, width, offset in ((slice(0, half), half, 0), (slice(half, tq), tq, half)):
            v = v_ref[pl.ds(q0, width), :]
            for r in range(2):
                s = jnp.where(_causal_mask(half, width, offset), scores(r, rows, q0, width), NEG_BIG)
                m_old = m_ref[r, rows, :]
                m_new = jnp.maximum(m_old, _row_max(s))
                alpha = jnp.exp2(m_old - m_new)
                p = jnp.exp2(s - m_new)
                l_ref[r, rows, :] = alpha * l_ref[r, rows, :] + _lane_partial_sum(p)
                acc_ref[r, rows, :] = alpha * acc_ref[r, rows, :] + _dot(p.astype(v.dtype), v)
                m_ref[r, rows, :] = m_new

    def qk(r, c):
        s_ref[r] = scores(r, slice(None), pl.multiple_of(c * tk, tk), tk)

    def softmax(r):
        s = s_ref[r]
        m_old = m_ref[r]
        m_new = jnp.maximum(m_old, _row_max(s))
        alpha = jnp.exp2(m_old - m_new)
        p = jnp.exp2(s - m_new)
        alpha_ref[r] = alpha
        l_ref[r] = alpha * l_ref[r] + _lane_partial_sum(p)
        p_ref[r] = p.astype(p_ref.dtype)
        m_ref[r] = m_new

    def pv(r, c):
        v = v_ref[pl.ds(pl.multiple_of(c * tk, tk), tk), :]
        acc_ref[r] = alpha_ref[r] * acc_ref[r] + _dot(p_ref[r], v)

    _two_stream_pipeline(qi, diagonal, qk, softmax, pv)

    lam_p = lam_ref[...]
    lam = (jnp.exp(jnp.sum(lam_p[0:1] * lam_p[1:2], axis=1, keepdims=True))
           - jnp.exp(jnp.sum(lam_p[2:3] * lam_p[3:4], axis=1, keepdims=True)) + lambda_init)
    l0 = jnp.sum(l_ref[0], axis=1, keepdims=True)
    l1 = jnp.sum(l_ref[1], axis=1, keepdims=True)
    o = acc_ref[0] / l0 - lam * (acc_ref[1] / l1)
    ms = jnp.mean(o * o, axis=-1, keepdims=True)
    o_ref[...] = (o * lax.rsqrt(ms + RMS_EPS) * g_ref[...] * (1.0 - lambda_init)).astype(o_ref.dtype)


def _diff_attention(qkv, pos_rows, lam_p, g, slopes, batch, seq, tq, lambda_init):
    m = qkv.shape[0]
    nh = DA_HEADS
    nq = seq // tq
    w = DA_V_DIM
    grid_spec = pltpu.PrefetchScalarGridSpec(
        num_scalar_prefetch=1,
        grid=(batch, nh, nq),
        in_specs=[pl.BlockSpec((tq, w), lambda b, h, qi, sl: (b * nq + qi, h)),
                  pl.BlockSpec((seq, w), lambda b, h, qi, sl: (b, nh + h)),
                  pl.BlockSpec((seq, w), lambda b, h, qi, sl: (b, 2 * nh + h)),
                  pl.BlockSpec((None, 1, seq), lambda b, h, qi, sl: (b, 0, 0)),
                  pl.BlockSpec((4, DA_HEAD_DIM), lambda b, h, qi, sl: (0, 0)),
                  pl.BlockSpec((1, w), lambda b, h, qi, sl: (0, 0))],
        out_specs=pl.BlockSpec((tq, w), lambda b, h, qi, sl: (b * nq + qi, h)),
        scratch_shapes=[pltpu.VMEM((2, tq, 1), F32),
                        pltpu.VMEM((2, tq, 1), F32),
                        pltpu.VMEM((2, tq, LANES), F32),
                        pltpu.VMEM((2, tq, w), F32),
                        pltpu.VMEM((2, tq, tq), F32),
                        pltpu.VMEM((2, tq, tq), BF16)],
    )
    return pl.pallas_call(
        functools.partial(_da_kernel, lambda_init=lambda_init),
        grid_spec=grid_spec,
        out_shape=jax.ShapeDtypeStruct((m, nh * w), BF16),
        compiler_params=_params(("parallel", "parallel", "arbitrary")),
        name="diff_attention",
    )(slopes, qkv, qkv, qkv, pos_rows, lam_p, g)


def _two_stream_pipeline(n, head, qk, softmax, pv):
    @pl.when(n == 0)
    def _():
        head()

    @pl.when(n > 0)
    def _():
        head()
        qk(0, 0)
        softmax(0)
        qk(1, 0)

        def trip(c, carry):
            pv(0, c - 1)
            softmax(1)
            qk(0, c)
            pv(1, c - 1)
            softmax(0)
            qk(1, c)
            return carry

        lax.fori_loop(1, n, trip, 0)
        pv(0, n - 1)
        softmax(1)
        pv(1, n - 1)


def _mla_kernel(q_ref, k_ref, v_ref, o_ref, m_ref, alpha_ref, acc_ref, s_ref, p_ref):
    qi = pl.program_id(2)
    tq = q_ref.shape[0]
    half = tq // 2

    m_ref[...] = jnp.full(m_ref.shape, NEG_BIG, F32)
    acc_ref[...] = jnp.zeros(acc_ref.shape, F32)

    def rows(r):
        return slice(r * half, (r + 1) * half)

    def diagonal(r, start, width, offset):
        s = _dot_nt(q_ref[rows(r), :], k_ref[pl.ds(start, width), :])
        s = jnp.where(_causal_mask(half, width, offset), s, NEG_BIG)
        m_old = m_ref[r]
        m_new = jnp.maximum(m_old, _row_max(s))
        p = jnp.exp2(s - m_new).astype(v_ref.dtype)
        acc_ref[r] = jnp.exp2(m_old - m_new) * acc_ref[r] + _dot(p, v_ref[pl.ds(start, width), :])
        m_ref[r] = m_new

    def qk(r, c):
        s_ref[r] = _dot_nt(q_ref[rows(r), :], k_ref[pl.ds(pl.multiple_of(c * tq, tq), tq), :])

    def softmax(r):
        s = s_ref[r]
        m_old = m_ref[r]
        m_new = jnp.maximum(m_old, _row_max(s))
        alpha_ref[r] = jnp.exp2(m_old - m_new)
        p_ref[r] = jnp.exp2(s - m_new).astype(p_ref.dtype)
        m_ref[r] = m_new

    def pv(r, c):
        v = v_ref[pl.ds(pl.multiple_of(c * tq, tq), tq), :]
        acc_ref[r] = alpha_ref[r] * acc_ref[r] + _dot(p_ref[r], v)

    d0 = pl.multiple_of(qi * tq, tq)

    def head():
        diagonal(0, d0, half, 0)
        diagonal(1, d0, tq, half)

    _two_stream_pipeline(qi, head, qk, softmax, pv)
    for r in range(2):
        acc = acc_ref[r]
        o_ref[rows(r), :] = (acc[:, :MLA_V_DIM] / acc[:, MLA_V_DIM:MLA_V_DIM + 1]).astype(o_ref.dtype)


def _mla_attention(qcat, kcat, vext, batch, seq, tq):
    m = qcat.shape[0]
    nh = MLA_HEADS
    nq = seq // tq
    w = 2 * LANES
    return pl.pallas_call(
        _mla_kernel,
        grid=(batch, nh, nq),
        in_specs=[pl.BlockSpec((tq, w), lambda b, h, qi: (b * nq + qi, h)),
                  pl.BlockSpec((seq, w), lambda b, h, qi: (b, h)),
                  pl.BlockSpec((seq, w), lambda b, h, qi: (b, h))],
        out_specs=pl.BlockSpec((tq, MLA_V_DIM), lambda b, h, qi: (b * nq + qi, h)),
        out_shape=jax.ShapeDtypeStruct((m, nh * MLA_V_DIM), BF16),
        scratch_shapes=[pltpu.VMEM((2, tq // 2, 1), F32),
                        pltpu.VMEM((2, tq // 2, 1), F32),
                        pltpu.VMEM((2, tq // 2, w), F32),
                        pltpu.VMEM((2, tq // 2, tq), F32),
                        pltpu.VMEM((2, tq // 2, tq), BF16)],
        compiler_params=_params(("parallel", "parallel", "arbitrary")),
        name="mla_attention",
    )(qcat, kcat, vext)


def _merge_kernel(x_ref, oa_ref, ob_ref, wga_ref, wgb_ref, wpa_ref, wpb_ref, ba_ref, bb_ref, o_ref):
    d = x_ref.shape[1]
    kc = oa_ref.shape[1] if d % oa_ref.shape[1] == 0 else d

    def gate(wt_ref, b_ref):
        acc = b_ref[...] + _dot_nt(x_ref[:, :kc], wt_ref[:, :kc])
        for c in range(kc, d, kc):
            acc = acc + _dot_nt(x_ref[:, c:c + kc], wt_ref[:, c:c + kc])
        return jax.nn.sigmoid(acc)

    ga = gate(wga_ref, ba_ref)
    gb = gate(wgb_ref, bb_ref)
    pa = _dot(oa_ref[...], wpa_ref[...])
    pb = _dot(ob_ref[...], wpb_ref[...])
    o_ref[...] = (ga * pa + gb * pb).astype(o_ref.dtype)


def _gated_merge(xb, oa, ob, wga_t, wgb_t, wpa, wpb, ba, bb, tm, tn):
    m, d = xb.shape
    n = wpa.shape[1]
    ko = oa.shape[1]
    return pl.pallas_call(
        _merge_kernel,
        grid=(m // tm, n // tn),
        in_specs=[pl.BlockSpec((tm, d), lambda i, j: (i, 0)),
                  pl.BlockSpec((tm, ko), lambda i, j: (i, 0)),
                  pl.BlockSpec((tm, ko), lambda i, j: (i, 0)),
                  pl.BlockSpec((tn, d), lambda i, j: (j, 0)),
                  pl.BlockSpec((tn, d), lambda i, j: (j, 0)),
                  pl.BlockSpec((ko, tn), lambda i, j: (0, j)),
                  pl.BlockSpec((ko, tn), lambda i, j: (0, j)),
                  pl.BlockSpec((1, tn), lambda i, j: (0, j)),
                  pl.BlockSpec((1, tn), lambda i, j: (0, j))],
        out_specs=pl.BlockSpec((tm, tn), lambda i, j: (i, j)),
        out_shape=jax.ShapeDtypeStruct((m, n), BF16),
        compiler_params=_params(("parallel", "arbitrary")),
        name="gated_merge",
    )(xb, oa, ob, wga_t, wgb_t, wpa, wpb, ba, bb)


def _mm_res_kernel(a_ref, w_ref, r_ref, o_ref, *, alpha):
    o_ref[...] = alpha * r_ref[...] + _dot(a_ref[...], w_ref[...])


def _mm_res(a, w, res, alpha, tm, tn):
    m, k = a.shape
    n = w.shape[1]
    return pl.pallas_call(
        functools.partial(_mm_res_kernel, alpha=alpha),
        grid=(m // tm, n // tn),
        in_specs=[pl.BlockSpec((tm, k), lambda i, j: (i, 0)),
                  pl.BlockSpec((k, tn), lambda i, j: (0, j)),
                  pl.BlockSpec((tm, tn), lambda i, j: (i, j))],
        out_specs=pl.BlockSpec((tm, tn), lambda i, j: (i, j)),
        out_shape=jax.ShapeDtypeStruct((m, n), F32),
        compiler_params=_params(("parallel", "arbitrary")),
        name="out_proj_residual",
    )(a, w, res)


def _layernorm(y, g, b):
    mu = jnp.mean(y, axis=-1, keepdims=True)
    yc = y - mu
    var = jnp.mean(yc * yc, axis=-1, keepdims=True)
    return yc * lax.rsqrt(var + LN_EPS) * g + b


def _ffn_up_kernel(y_ref, g_ref, b_ref, w_ref, cw_ref, cb_ref, o_ref, h_ref, u_ref, carry_ref, *, nj, tiles_per_seq):
    t = pl.program_id(0)
    last = pl.num_programs(0) - 1
    tm = y_ref.shape[0]
    half = o_ref.shape[1]
    pad = SUBLANES

    @pl.when(t == 0)
    def _():
        u_ref[...] = jnp.zeros(u_ref.shape, F32)
        carry_ref[...] = jnp.zeros(carry_ref.shape, F32)

    @pl.when((t % nj == 0) & (t < last))
    def _():
        h_ref[...] = _layernorm(y_ref[...], g_ref[...], b_ref[...]).astype(h_ref.dtype)

    tp = jnp.maximum(t - 1, 0)
    ip = tp // nj
    jp = tp % nj
    seq_start = (ip % tiles_per_seq == 0).astype(F32)
    u_ref[:pad, :] = carry_ref[jp] * (1.0 - seq_start)
    cw = cw_ref[...]
    c = cb_ref[...] + cw[2:3] * u_ref[pad:, :]
    for back in range(1, CONV_WIDTH):
        c = c + cw[CONV_WIDTH - 1 - back:CONV_WIDTH - back] * u_ref[pad - back:pad - back + tm, :]
    carry_ref[jp] = u_ref[tm:, :]
    gate = c[:, :half]
    o_ref[...] = (gate * jax.nn.sigmoid(gate) * c[:, half:]).astype(o_ref.dtype)

    u_ref[pad:, :] = _dot(h_ref[...], w_ref[...])


def _ffn_up(y, g, b, w_up, conv_w, conv_b, seq, tm, tn):
    m, d = y.shape
    n2 = w_up.shape[1]
    nj = n2 // (2 * tn)
    steps = (m // tm) * nj

    def cur(t):
        return jnp.minimum(t, steps - 1)

    def prev(t):
        return jnp.maximum(t - 1, 0)

    return pl.pallas_call(
        functools.partial(_ffn_up_kernel, nj=nj, tiles_per_seq=seq // tm),
        grid=(steps + 1,),
        in_specs=[pl.BlockSpec((tm, d), lambda t: (cur(t) // nj, 0)),
                  pl.BlockSpec((1, d), lambda t: (0, 0)),
                  pl.BlockSpec((1, d), lambda t: (0, 0)),
                  pl.BlockSpec((d, 2 * tn), lambda t: (0, cur(t) % nj)),
                  pl.BlockSpec((CONV_WIDTH, 2 * tn), lambda t: (0, prev(t) % nj)),
                  pl.BlockSpec((1, 2 * tn), lambda t: (0, prev(t) % nj))],
        out_specs=pl.BlockSpec((tm, tn), lambda t: (prev(t) // nj, prev(t) % nj)),
        out_shape=jax.ShapeDtypeStruct((m, n2 // 2), BF16),
        scratch_shapes=[pltpu.VMEM((tm, d), BF16),
                        pltpu.VMEM((tm + SUBLANES, 2 * tn), F32),
                        pltpu.VMEM((nj, SUBLANES, 2 * tn), F32)],
        compiler_params=_params(("arbitrary",)),
        name="ffn_up_conv_gate",
    )(y, g, b, w_up, conv_w, conv_b)


def _ffn_down_kernel(a_ref, w_ref, y_ref, g1_ref, b1_ref, g2_ref, b2_ref, o_ref, *, alpha):
    k = pl.program_id(1)
    tm, d = o_ref.shape

    @pl.when(k == 0)
    def _():
        o_ref[...] = jnp.zeros(o_ref.shape, F32)

    a = a_ref[...]
    for c in range(0, d, FFN_DOWN_COL_CHUNK):
        cols = slice(c, min(c + FFN_DOWN_COL_CHUNK, d))
        o_ref[:, cols] += _dot(a, w_ref[:, cols])

    @pl.when(k == pl.num_programs(1) - 1)
    def _():
        rows_per = min(tm, FFN_DOWN_ROW_CHUNK)

        def body(r, carry):
            rows = pl.ds(pl.multiple_of(r * rows_per, rows_per), rows_per)
            h = _layernorm(y_ref[rows, :], g1_ref[...], b1_ref[...])
            o_ref[rows, :] = _layernorm(alpha * h + o_ref[rows, :], g2_ref[...], b2_ref[...])
            return carry

        lax.fori_loop(0, tm // rows_per, body, 0)


def _ffn_down(act, w_down, y, g1, b1, g2, b2, alpha, tm, tk):
    m, kdim = act.shape
    d = w_down.shape[1]
    vec = pl.BlockSpec((1, d), lambda i, k: (0, 0))
    return pl.pallas_call(
        functools.partial(_ffn_down_kernel, alpha=alpha),
        grid=(m // tm, kdim // tk),
        in_specs=[pl.BlockSpec((tm, tk), lambda i, k: (i, k)),
                  pl.BlockSpec((tk, d), lambda i, k: (k, 0)),
                  pl.BlockSpec((tm, d), lambda i, k: (i, 0)),
                  vec, vec, vec, vec],
        out_specs=pl.BlockSpec((tm, d), lambda i, k: (i, 0)),
        out_shape=jax.ShapeDtypeStruct((m, d), F32),
        compiler_params=_params(("parallel", "arbitrary")),
        name="ffn_down_layernorm",
    )(act, w_down, y, g1, b1, g2, b2)


def _cast_kernel(x_ref, o_ref):
    o_ref[...] = x_ref[...].astype(o_ref.dtype)


def _cast_bf16(w, rb, cb, name):
    rows, cols = w.shape
    return pl.pallas_call(
        _cast_kernel,
        grid=(pl.cdiv(rows, rb), pl.cdiv(cols, cb)),
        in_specs=[pl.BlockSpec((rb, cb), lambda i, j: (i, j))],
        out_specs=pl.BlockSpec((rb, cb), lambda i, j: (i, j)),
        out_shape=jax.ShapeDtypeStruct((rows, cols), BF16),
        compiler_params=_params(("parallel", "parallel")),
        name=name,
    )(w)


def _cast_pad_kernel(x_ref, o_ref, *, n_src):
    keep = pl.program_id(0) < n_src
    o_ref[...] = jnp.where(keep, x_ref[...], 0.0).astype(o_ref.dtype)


def _cast_pad_rows_bf16(w, rows_out, rb, name):
    rows, cols = w.shape
    n_src = rows // rb
    return pl.pallas_call(
        functools.partial(_cast_pad_kernel, n_src=n_src),
        grid=(rows_out // rb,),
        in_specs=[pl.BlockSpec((rb, cols), lambda i: (jnp.minimum(i, n_src - 1), 0))],
        out_specs=pl.BlockSpec((rb, cols), lambda i: (i, 0)),
        out_shape=jax.ShapeDtypeStruct((rows_out, cols), BF16),
        compiler_params=_params(("parallel",)),
        name=name,
    )(w)


def _regroup_kernel(src_ref, keep_ref, *refs):
    x_refs, o_ref = refs[:-1], refs[-1]
    cb = x_refs[0].shape[1]
    for k, x_ref in enumerate(x_refs):
        keep = keep_ref[pl.program_id(0) * len(x_refs) + k] > 0
        o_ref[:, k * cb:(k + 1) * cb] = jnp.where(keep, x_ref[...], 0.0).astype(o_ref.dtype)


def _regroup_cast_w_up(w_up, f, tn, nj):
    d = w_up.shape[0]
    cb = math.gcd(f, tn)
    sub = tn // cb
    nsrc = f // cb
    src, keep = [], []
    for j in range(nj):
        for base in (0, nsrc):
            for r in range(sub):
                g = j * sub + r
                src.append(base + min(g, nsrc - 1))
                keep.append(1 if g < nsrc else 0)
    per_step = 2 * sub

    def src_spec(k):
        return pl.BlockSpec((d, cb), lambda b, src, keep: (0, src[b * per_step + k]))

    grid_spec = pltpu.PrefetchScalarGridSpec(
        num_scalar_prefetch=2,
        grid=(nj,),
        in_specs=[src_spec(k) for k in range(per_step)],
        out_specs=pl.BlockSpec((d, per_step * cb), lambda b, src, keep: (0, b)),
    )
    return pl.pallas_call(
        _regroup_kernel,
        grid_spec=grid_spec,
        out_shape=jax.ShapeDtypeStruct((d, nj * 2 * tn), BF16),
        compiler_params=_params(("parallel",)),
        name="w_up_regroup_cast",
    )(jnp.asarray(src, jnp.int32), jnp.asarray(keep, jnp.int32), *([w_up] * per_step))


def _rope_rows(wt):
    half = MLA_ROPE_DIM // 2
    return jnp.concatenate([-wt[half:], wt[:half]], axis=0)


def _pad_rows(w, n):
    return jnp.pad(w, ((0, n - w.shape[0]), (0, 0)))


def _pad_cols(w, n):
    return jnp.pad(w, ((0, 0), (0, n - w.shape[1])))


class _Tiles(NamedTuple):
    proj_m: int
    proj_n: int
    kv_up_m: int
    attn_q: int
    merge_m: int
    merge_n: int
    ffn_m: int
    ffn_n: int
    ffn_k: int
    cast: int


def _plan_tiles(m, d, seq, fp):
    return _Tiles(proj_m=_tile(m, 1024), proj_n=_tile(d, 1024), kv_up_m=_tile(m, 1024), attn_q=_tile(seq, 1024),
                  merge_m=_tile(m, 512), merge_n=_tile(d, 512), ffn_m=_tile(seq, 512), ffn_n=FFN_COLS_PER_STEP,
                  ffn_k=_tile(fp, 1024), cast=1024)


def _layer(h, posb, pos_rows, batch, seq, alpha, lambda_init, w_in, b_gate, lam_p, da_g, q_g, kv_g, w_uq, w_ukv,
           w_proj_a, w_proj_b, w_out, ln1_g, ln1_b, w_up, conv_w, conv_b, w_down, ln2_g, ln2_b):
    m, d = h.shape
    f = w_down.shape[0]
    fp = -(-f // FFN_COLS_PER_STEP) * FFN_COLS_PER_STEP
    t = _plan_tiles(m, d, seq, fp)
    da_cols = 3 * DA_HEADS * DA_V_DIM
    off_cq = da_cols
    off_ckv = off_cq + MLA_Q_RANK
    off_kr = off_ckv + MLA_KV_RANK
    off_gate = off_kr + MLA_ROPE_DIM

    hb = h.astype(BF16)

    da_scale = DA_HEAD_DIM ** -0.5 * LOG2E
    qkv_scale = jnp.concatenate([jnp.full((DA_HEADS * DA_V_DIM,), da_scale, F32),
                                 jnp.ones((2 * DA_HEADS * DA_V_DIM,), F32)])[None, :]
    wt_in = _cast_bf16(jnp.swapaxes(w_in, 0, 1), t.cast, d, "w_in_cast")
    qkv = _mm_scale(hb, wt_in, qkv_scale, BF16, t.proj_m, t.proj_n, "da_qkv_proj")

    wt_kr = wt_in[off_kr:off_gate]
    wt_lat = jnp.concatenate([wt_in[off_cq:off_kr], _pad_rows(wt_kr, LANES), _pad_rows(_rope_rows(wt_kr), LANES)], axis=0)
    lat = _mm_scale(hb, wt_lat, jnp.ones((1, wt_lat.shape[0]), F32), F32, t.proj_m, wt_lat.shape[0] // 2,
                    "mla_latent_proj")

    slopes = 2.0 ** (-8.0 * jnp.arange(1, DA_HEADS + 1, dtype=F32) / DA_HEADS)
    oa = _diff_attention(qkv, pos_rows, lam_p, da_g[None, :], slopes, batch, seq, t.attn_q, lambda_init)

    half = MLA_ROPE_DIM // 2
    inv = ROPE_THETA ** (-jnp.arange(half, dtype=F32) / half)
    inv_row = _pad_cols(jnp.concatenate([inv, inv])[None, :], LANES)
    hd = MLA_NOPE_DIM + MLA_ROPE_DIM
    wq = w_uq.reshape(MLA_Q_RANK, MLA_HEADS, hd)
    zeros = jnp.zeros((MLA_Q_RANK, MLA_HEADS, LANES - MLA_ROPE_DIM), F32)
    wq_main = jnp.concatenate([wq, zeros], axis=2).reshape(MLA_Q_RANK, MLA_HEADS * 2 * LANES).astype(BF16)
    wq_swap = jnp.concatenate([-wq[:, :, MLA_NOPE_DIM + half:], wq[:, :, MLA_NOPE_DIM:MLA_NOPE_DIM + half], zeros],
                              axis=2).reshape(MLA_Q_RANK, MLA_HEADS * LANES).astype(BF16)
    qcat = _q_up(lat, q_g[None, :], posb, inv_row, wq_main, wq_swap, hd ** -0.5 * LOG2E, t.proj_m)
    kcat, vext = _kv_up(lat, kv_g[None, :], posb, inv_row, w_ukv.astype(BF16), t.kv_up_m)
    ob = _mla_attention(qcat, kcat, vext, batch, seq, t.attn_q)

    merged = _gated_merge(hb, oa, ob, wt_in[off_gate:off_gate + d], wt_in[off_gate + d:],
                          w_proj_a.astype(BF16), w_proj_b.astype(BF16), b_gate[None, :d], b_gate[None, d:],
                          t.merge_m, t.merge_n)
    y1 = _mm_res(merged, w_out.astype(BF16), h, alpha, t.proj_m, t.proj_n)

    tn = t.ffn_n
    nj = fp // tn

    def group(a):
        r = a.shape[0]
        gv = jnp.stack([_pad_cols(a[:, :f], fp).reshape(r, nj, tn), _pad_cols(a[:, f:], fp).reshape(r, nj, tn)], axis=2)
        return gv.reshape(r, nj * 2 * tn)

    act = _ffn_up(y1, ln1_g[None, :], ln1_b[None, :], _regroup_cast_w_up(w_up, f, tn, nj), group(conv_w),
                  group(conv_b[None, :]), seq, t.ffn_m, tn)
    w_down_p = _cast_pad_rows_bf16(w_down, fp, math.gcd(f, tn), "w_down_cast")
    return _ffn_down(act, w_down_p, y1, ln1_g[None, :], ln1_b[None, :], ln2_g[None, :], ln2_b[None, :], alpha,
                     t.ffn_m, t.ffn_k)


def kernel(x, positions, w_in, b_gate, da_lambda_q1, da_lambda_k1, da_lambda_q2, da_lambda_k2, da_subln_g, mla_q_norm_g, mla_kv_norm_g, w_uq, w_ukv, w_proj_a, w_proj_b, w_out, ln1_g, ln1_b, w_up, conv_w, conv_b, w_down, ln2_g, ln2_b):
    batch, seq, d = x.shape
    depth = w_in.shape[0]
    alpha = (2.0 * depth) ** 0.25
    m = batch * seq
    posf = positions.astype(F32).reshape(m)
    posb = jnp.broadcast_to(posf[:, None], (m, LANES))
    pos_rows = posf.reshape(batch, 1, seq)
    h = x.reshape(m, d)
    for l in range(depth):
        lambda_init = 0.8 - 0.6 * math.exp(-0.3 * l)
        lam_p = jnp.stack([da_lambda_q1[l], da_lambda_k1[l], da_lambda_q2[l], da_lambda_k2[l]]).astype(F32)
        h = _layer(h, posb, pos_rows, batch, seq, alpha, lambda_init, w_in[l], b_gate[l], lam_p, da_subln_g[l],
                   mla_q_norm_g[l], mla_kv_norm_g[l], w_uq[l], w_ukv[l], w_proj_a[l], w_proj_b[l], w_out[l],
                   ln1_g[l], ln1_b[l], w_up[l], conv_w[l], conv_b[l], w_down[l], ln2_g[l], ln2_b[l])
    return h.reshape(batch, seq, d)
```

```python
import functools
import math
from typing import NamedTuple

import numpy as np
import jax
import jax.numpy as jnp
from jax import lax
from jax.experimental import pallas as pl
from jax.experimental.pallas import tpu as pltpu

DA_HEADS = 8
DA_HEAD_DIM = 128
DA_V_DIM = 2 * DA_HEAD_DIM
MLA_HEADS = 16
MLA_Q_RANK = 1024
MLA_KV_RANK = 512
MLA_NOPE_DIM = 128
MLA_ROPE_DIM = 64
MLA_V_DIM = 128
ROPE_THETA = 10000.0
CONV_WIDTH = 3
LN_EPS = 1e-5
RMS_EPS = 1e-6

LANES = 128
SUBLANES = 8
VMEM_LIMIT = 56 * 1024 * 1024
NEG_BIG = -1e30
LOG2E = math.log2(math.e)
Q_UP_HEADS_PER_STEP = 8
FFN_COLS_PER_STEP = 512
FFN_DOWN_COL_CHUNK = 1024
FFN_DOWN_ROW_CHUNK = 64

F32 = jnp.float32
BF16 = jnp.bfloat16


def _params(sem):
    return pltpu.CompilerParams(dimension_semantics=sem, vmem_limit_bytes=VMEM_LIMIT)


def _tile(n, pref):
    t = min(n, pref)
    while n % t:
        t //= 2
    return t


def _dot(a, b):
    return jnp.dot(a, b, preferred_element_type=F32)


def _dot_nt(a, b):
    return lax.dot_general(a, b, (((1,), (1,)), ((), ())), preferred_element_type=F32)


def _mm_scale_kernel(a_ref, wt_ref, s_ref, o_ref):
    o_ref[...] = (_dot_nt(a_ref[...], wt_ref[...]) * s_ref[...]).astype(o_ref.dtype)


def _mm_scale(a, wt, scale, out_dtype, tm, tn, name):
    m, k = a.shape
    n = scale.shape[1]
    return pl.pallas_call(
        _mm_scale_kernel,
        grid=(m // tm, n // tn),
        in_specs=[pl.BlockSpec((tm, k), lambda i, j: (i, 0)),
                  pl.BlockSpec((tn, k), lambda i, j: (j, 0)),
                  pl.BlockSpec((1, tn), lambda i, j: (0, j))],
        out_specs=pl.BlockSpec((tm, tn), lambda i, j: (i, j)),
        out_shape=jax.ShapeDtypeStruct((m, n), out_dtype),
        compiler_params=_params(("parallel", "arbitrary")),
        name=name,
    )(a, wt, scale)


def _rms_bf16(c, g):
    ms = jnp.mean(c * c, axis=-1, keepdims=True)
    return (c * lax.rsqrt(ms + RMS_EPS) * g).astype(BF16)


def _qup_kernel(cq_ref, g_ref, pos_ref, inv_ref, wm_ref, ws_ref, o_ref, cn_ref, cos_ref, sin_ref, *, scale):
    @pl.when(pl.program_id(1) == 0)
    def _():
        cn_ref[...] = _rms_bf16(cq_ref[...], g_ref[...])
        ang = pos_ref[...] * inv_ref[...]
        cos_ref[...] = jnp.cos(ang)
        sin_ref[...] = jnp.sin(ang)

    cn = cn_ref[...]
    main = _dot(cn, wm_ref[...])
    swapped = _dot(cn, ws_ref[...])
    for hh in range(o_ref.shape[1] // (2 * LANES)):
        c0 = hh * 2 * LANES
        rope = main[:, c0 + LANES:c0 + 2 * LANES] * cos_ref[...] + swapped[:, hh * LANES:(hh + 1) * LANES] * sin_ref[...]
        o_ref[:, c0:c0 + LANES] = (main[:, c0:c0 + LANES] * scale).astype(o_ref.dtype)
        o_ref[:, c0 + LANES:c0 + 2 * LANES] = (rope * scale).astype(o_ref.dtype)


def _q_up(lat, g, posb, inv_row, w_main, w_swap, scale, tm):
    m = lat.shape[0]
    nh = MLA_HEADS
    hg = Q_UP_HEADS_PER_STEP
    return pl.pallas_call(
        functools.partial(_qup_kernel, scale=scale),
        grid=(m // tm, nh // hg),
        in_specs=[pl.BlockSpec((tm, MLA_Q_RANK), lambda i, h: (i, 0)),
                  pl.BlockSpec((1, MLA_Q_RANK), lambda i, h: (0, 0)),
                  pl.BlockSpec((tm, LANES), lambda i, h: (i, 0)),
                  pl.BlockSpec((1, LANES), lambda i, h: (0, 0)),
                  pl.BlockSpec((MLA_Q_RANK, hg * 2 * LANES), lambda i, h: (0, h)),
                  pl.BlockSpec((MLA_Q_RANK, hg * LANES), lambda i, h: (0, h))],
        out_specs=pl.BlockSpec((tm, hg * 2 * LANES), lambda i, h: (i, h)),
        out_shape=jax.ShapeDtypeStruct((m, nh * 2 * LANES), BF16),
        scratch_shapes=[pltpu.VMEM((tm, MLA_Q_RANK), BF16),
                        pltpu.VMEM((tm, LANES), F32),
                        pltpu.VMEM((tm, LANES), F32)],
        compiler_params=_params(("parallel", "arbitrary")),
        name="mla_q_up",
    )(lat, g, posb, inv_row, w_main, w_swap)


def _kvup_kernel(ckv_ref, kr_ref, krs_ref, g_ref, pos_ref, inv_ref, w_ref, k_ref, v_ref):
    tm = ckv_ref.shape[0]
    cn = _rms_bf16(ckv_ref[...], g_ref[...])
    ang = pos_ref[...] * inv_ref[...]
    krope = (kr_ref[...] * jnp.cos(ang) + krs_ref[...] * jnp.sin(ang)).astype(k_ref.dtype)
    ones_col = (lax.broadcasted_iota(jnp.int32, (tm, LANES), 1) == 0).astype(v_ref.dtype)
    wh = MLA_NOPE_DIM + MLA_V_DIM
    for h in range(MLA_HEADS):
        kv = _dot(cn, w_ref[:, h * wh:(h + 1) * wh])
        k_ref[:, h * wh:h * wh + MLA_NOPE_DIM] = kv[:, :MLA_NOPE_DIM].astype(k_ref.dtype)
        k_ref[:, h * wh + MLA_NOPE_DIM:(h + 1) * wh] = krope
        v_ref[:, h * wh:h * wh + MLA_V_DIM] = kv[:, MLA_NOPE_DIM:].astype(v_ref.dtype)
        v_ref[:, h * wh + MLA_V_DIM:(h + 1) * wh] = ones_col


def _kv_up(lat, g, posb, inv_row, w_ukv, tm):
    m = lat.shape[0]
    n = w_ukv.shape[1]
    ckv_blk = MLA_Q_RANK // MLA_KV_RANK
    kr_blk = (MLA_Q_RANK + MLA_KV_RANK) // LANES
    return pl.pallas_call(
        _kvup_kernel,
        grid=(m // tm,),
        in_specs=[pl.BlockSpec((tm, MLA_KV_RANK), lambda i: (i, ckv_blk)),
                  pl.BlockSpec((tm, LANES), lambda i: (i, kr_blk)),
                  pl.BlockSpec((tm, LANES), lambda i: (i, kr_blk + 1)),
                  pl.BlockSpec((1, MLA_KV_RANK), lambda i: (0, 0)),
                  pl.BlockSpec((tm, LANES), lambda i: (i, 0)),
                  pl.BlockSpec((1, LANES), lambda i: (0, 0)),
                  pl.BlockSpec((MLA_KV_RANK, n), lambda i: (0, 0))],
        out_specs=[pl.BlockSpec((tm, n), lambda i: (i, 0)),
                   pl.BlockSpec((tm, n), lambda i: (i, 0))],
        out_shape=[jax.ShapeDtypeStruct((m, n), BF16),
                   jax.ShapeDtypeStruct((m, n), BF16)],
        compiler_params=_params(("parallel",)),
        name="mla_kv_up",
    )(lat, lat, lat, g, posb, inv_row, w_ukv)


def _row_max(s):
    part = s[:, :LANES]
    for c in range(LANES, s.shape[1], LANES):
        part = jnp.maximum(part, s[:, c:c + LANES])
    return jnp.max(part, axis=1, keepdims=True)


def _lane_partial_sum(p):
    part = p[:, :LANES]
    for c in range(LANES, p.shape[1], LANES):
        part = part + p[:, c:c + LANES]
    return part


def _causal_mask(rows, cols, offset):
    row = lax.broadcasted_iota(jnp.int32, (rows, cols), 0)
    col = lax.broadcasted_iota(jnp.int32, (rows, cols), 1)
    return col <= row + offset


def _da_kernel(slope_ref, q_ref, k_ref, v_ref, pos_ref, lam_ref, g_ref, o_ref,
               m_ref, alpha_ref, l_ref, acc_ref, s_ref, p_ref, *, lambda_init):
    h = pl.program_id(1)
    qi = pl.program_id(2)
    d = DA_HEAD_DIM
    tq = q_ref.shape[0]
    tk = s_ref.shape[2]

    m_ref[...] = jnp.full(m_ref.shape, NEG_BIG, F32)
    l_ref[...] = jnp.zeros(l_ref.shape, F32)
    acc_ref[...] = jnp.zeros(acc_ref.shape, F32)

    slope2 = slope_ref[h] * LOG2E
    q0 = pl.multiple_of(qi * tq, tq)
    pq0 = pos_ref[:, pl.ds(q0, LANES)][:, :1]

    def scores(r, rows, start, width):
        cols = slice(r * d, (r + 1) * d)
        bias = slope2 * (pos_ref[:, pl.ds(start, width)] - pq0)
        return _dot_nt(q_ref[rows, cols], k_ref[pl.ds(start, width), cols]) + bias

    def diagonal():
        half = tq // 2
        for rows, width, offset in ((slice(0, half), half, 0), (slice(half, tq), tq, half)):
            v = v_ref[pl.ds(q0, width), :]
            for r in range(2):
                s = jnp.where(_causal_mask(half, width, offset), scores(r, rows, q0, width), NEG_BIG)
                m_old = m_ref[r, rows, :]
                m_new = jnp.maximum(m_old, _row_max(s))
                alpha = jnp.exp2(m_old - m_new)
                p = jnp.exp2(s - m_new)
                l_ref[r, rows, :] = alpha * l_ref[r, rows, :] + _lane_partial_sum(p)
                acc_ref[r, rows, :] = alpha * acc_ref[r, rows, :] + _dot(p.astype(v.dtype), v)
                m_ref[r, rows, :] = m_new

    def qk(r, c):
        s_ref[r] = scores(r, slice(None), pl.multiple_of(c * tk, tk), tk)

    def softmax(r):
        s = s_ref[r]
        m_old = m_ref[r]
        m_new = jnp.maximum(m_old, _row_max(s))
        alpha = jnp.exp2(m_old - m_new)
        p = jnp.exp2(s - m_new)
        alpha_ref[r] = alpha
        l_ref[r] = alpha * l_ref[r] + _lane_partial_sum(p)
        p_ref[r] = p.astype(p_ref.dtype)
        m_ref[r] = m_new

    def pv(r, c):
        v = v_ref[pl.ds(pl.multiple_of(c * tk, tk), tk), :]
        acc_ref[r] = alpha_ref[r] * acc_ref[r] + _dot(p_ref[r], v)

    _two_stream_pipeline(qi, diagonal, qk, softmax, pv)

    lam_p = lam_ref[...]
    lam = (jnp.exp(jnp.sum(lam_p[0:1] * lam_p[1:2], axis=1, keepdims=True))
           - jnp.exp(jnp.sum(lam_p[2:3] * lam_p[3:4], axis=1, keepdims=True)) + lambda_init)
    l0 = jnp.sum(l_ref[0], axis=1, keepdims=True)
    l1 = jnp.sum(l_ref[1], axis=1, keepdims=True)
    o = acc_ref[0] / l0 - lam * (acc_ref[1] / l1)
    ms = jnp.mean(o * o, axis=-1, keepdims=True)
    o_ref[...] = (o * lax.rsqrt(ms + RMS_EPS) * g_ref[...] * (1.0 - lambda_init)).astype(o_ref.dtype)


def _diff_attention(qkv, pos_rows, lam_p, g, slopes, batch, seq, tq, lambda_init):
    m = qkv.shape[0]
    nh = DA_HEADS
    nq = seq // tq
    w = DA_V_DIM
    grid_spec = pltpu.PrefetchScalarGridSpec(
        num_scalar_prefetch=1,
        grid=(batch, nh, nq),
        in_specs=[pl.BlockSpec((tq, w), lambda b, h, qi, sl: (b * nq + qi, h)),
                  pl.BlockSpec((seq, w), lambda b, h, qi, sl: (b, nh + h)),
                  pl.BlockSpec((seq, w), lambda b, h, qi, sl: (b, 2 * nh + h)),
                  pl.BlockSpec((None, 1, seq), lambda b, h, qi, sl: (b, 0, 0)),
                  pl.BlockSpec((4, DA_HEAD_DIM), lambda b, h, qi, sl: (0, 0)),
                  pl.BlockSpec((1, w), lambda b, h, qi, sl: (0, 0))],
        out_specs=pl.BlockSpec((tq, w), lambda b, h, qi, sl: (b * nq + qi, h)),
        scratch_shapes=[pltpu.VMEM((2, tq, 1), F32),
                        pltpu.VMEM((2, tq, 1), F32),
                        pltpu.VMEM((2, tq, LANES), F32),
                        pltpu.VMEM((2, tq, w), F32),
                        pltpu.VMEM((2, tq, tq), F32),
                        pltpu.VMEM((2, tq, tq), BF16)],
    )
    return pl.pallas_call(
        functools.partial(_da_kernel, lambda_init=lambda_init),
        grid_spec=grid_spec,
        out_shape=jax.ShapeDtypeStruct((m, nh * w), BF16),
        compiler_params=_params(("parallel", "parallel", "arbitrary")),
        name="diff_attention",
    )(slopes, qkv, qkv, qkv, pos_rows, lam_p, g)


def _two_stream_pipeline(n, head, qk, softmax, pv):
    @pl.when(n == 0)
    def _():
        head()

    @pl.when(n > 0)
    def _():
        head()
        qk(0, 0)
        softmax(0)
        qk(1, 0)

        def trip(c, carry):
            pv(0, c - 1)
            softmax(1)
            qk(0, c)
            pv(1, c - 1)
            softmax(0)
            qk(1, c)
            return carry

        lax.fori_loop(1, n, trip, 0)
        pv(0, n - 1)
        softmax(1)
        pv(1, n - 1)


def _mla_kernel(q_ref, k_ref, v_ref, o_ref, m_ref, alpha_ref, acc_ref, s_ref, p_ref):
    qi = pl.program_id(2)
    tq = q_ref.shape[0]
    half = tq // 2

    m_ref[...] = jnp.full(m_ref.shape, NEG_BIG, F32)
    acc_ref[...] = jnp.zeros(acc_ref.shape, F32)

    def rows(r):
        return slice(r * half, (r + 1) * half)

    def diagonal(r, start, width, offset):
        s = _dot_nt(q_ref[rows(r), :], k_ref[pl.ds(start, width), :])
        s = jnp.where(_causal_mask(half, width, offset), s, NEG_BIG)
        m_old = m_ref[r]
        m_new = jnp.maximum(m_old, _row_max(s))
        p = jnp.exp2(s - m_new).astype(v_ref.dtype)
        acc_ref[r] = jnp.exp2(m_old - m_new) * acc_ref[r] + _dot(p, v_ref[pl.ds(start, width), :])
        m_ref[r] = m_new

    def qk(r, c):
        s_ref[r] = _dot_nt(q_ref[rows(r), :], k_ref[pl.ds(pl.multiple_of(c * tq, tq), tq), :])

    def softmax(r):
        s = s_ref[r]
        m_old = m_ref[r]
        m_new = jnp.maximum(m_old, _row_max(s))
        alpha_ref[r] = jnp.exp2(m_old - m_new)
        p_ref[r] = jnp.exp2(s - m_new).astype(p_ref.dtype)
        m_ref[r] = m_new

    def pv(r, c):
        v = v_ref[pl.ds(pl.multiple_of(c * tq, tq), tq), :]
        acc_ref[r] = alpha_ref[r] * acc_ref[r] + _dot(p_ref[r], v)

    d0 = pl.multiple_of(qi * tq, tq)

    def head():
        diagonal(0, d0, half, 0)
        diagonal(1, d0, tq, half)

    _two_stream_pipeline(qi, head, qk, softmax, pv)
    for r in range(2):
        acc = acc_ref[r]
        o_ref[rows(r), :] = (acc[:, :MLA_V_DIM] / acc[:, MLA_V_DIM:MLA_V_DIM + 1]).astype(o_ref.dtype)


def _mla_attention(qcat, kcat, vext, batch, seq, tq):
    m = qcat.shape[0]
    nh = MLA_HEADS
    nq = seq // tq
    w = 2 * LANES
    return pl.pallas_call(
        _mla_kernel,
        grid=(batch, nh, nq),
        in_specs=[pl.BlockSpec((tq, w), lambda b, h, qi: (b * nq + qi, h)),
                  pl.BlockSpec((seq, w), lambda b, h, qi: (b, h)),
                  pl.BlockSpec((seq, w), lambda b, h, qi: (b, h))],
        out_specs=pl.BlockSpec((tq, MLA_V_DIM), lambda b, h, qi: (b * nq + qi, h)),
        out_shape=jax.ShapeDtypeStruct((m, nh * MLA_V_DIM), BF16),
        scratch_shapes=[pltpu.VMEM((2, tq // 2, 1), F32),
                        pltpu.VMEM((2, tq // 2, 1), F32),
                        pltpu.VMEM((2, tq // 2, w), F32),
                        pltpu.VMEM((2, tq // 2, tq), F32),
                        pltpu.VMEM((2, tq // 2, tq), BF16)],
        compiler_params=_params(("parallel", "parallel", "arbitrary")),
        name="mla_attention",
    )(qcat, kcat, vext)


def _merge_kernel(x_ref, oa_ref, ob_ref, wga_ref, wgb_ref, wpa_ref, wpb_ref, ba_ref, bb_ref, o_ref):
    d = x_ref.shape[1]
    kc = oa_ref.shape[1] if d % oa_ref.shape[1] == 0 else d

    def gate(wt_ref, b_ref):
        acc = b_ref[...] + _dot_nt(x_ref[:, :kc], wt_ref[:, :kc])
        for c in range(kc, d, kc):
            acc = acc + _dot_nt(x_ref[:, c:c + kc], wt_ref[:, c:c + kc])
        return jax.nn.sigmoid(acc)

    ga = gate(wga_ref, ba_ref)
    gb = gate(wgb_ref, bb_ref)
    pa = _dot(oa_ref[...], wpa_ref[...])
    pb = _dot(ob_ref[...], wpb_ref[...])
    o_ref[...] = (ga * pa + gb * pb).astype(o_ref.dtype)


def _gated_merge(xb, oa, ob, wga_t, wgb_t, wpa, wpb, ba, bb, tm, tn):
    m, d = xb.shape
    n = wpa.shape[1]
    ko = oa.shape[1]
    return pl.pallas_call(
        _merge_kernel,
        grid=(m // tm, n // tn),
        in_specs=[pl.BlockSpec((tm, d), lambda i, j: (i, 0)),
                  pl.BlockSpec((tm, ko), lambda i, j: (i, 0)),
                  pl.BlockSpec((tm, ko), lambda i, j: (i, 0)),
                  pl.BlockSpec((tn, d), lambda i, j: (j, 0)),
                  pl.BlockSpec((tn, d), lambda i, j: (j, 0)),
                  pl.BlockSpec((ko, tn), lambda i, j: (0, j)),
                  pl.BlockSpec((ko, tn), lambda i, j: (0, j)),
                  pl.BlockSpec((1, tn), lambda i, j: (0, j)),
                  pl.BlockSpec((1, tn), lambda i, j: (0, j))],
        out_specs=pl.BlockSpec((tm, tn), lambda i, j: (i, j)),
        out_shape=jax.ShapeDtypeStruct((m, n), BF16),
        compiler_params=_params(("parallel", "arbitrary")),
        name="gated_merge",
    )(xb, oa, ob, wga_t, wgb_t, wpa, wpb, ba, bb)


def _mm_res_kernel(a_ref, w_ref, r_ref, o_ref, *, alpha):
    o_ref[...] = alpha * r_ref[...] + _dot(a_ref[...], w_ref[...])


def _mm_res(a, w, res, alpha, tm, tn):
    m, k = a.shape
    n = w.shape[1]
    return pl.pallas_call(
        functools.partial(_mm_res_kernel, alpha=alpha),
        grid=(m // tm, n // tn),
        in_specs=[pl.BlockSpec((tm, k), lambda i, j: (i, 0)),
                  pl.BlockSpec((k, tn), lambda i, j: (0, j)),
                  pl.BlockSpec((tm, tn), lambda i, j: (i, j))],
        out_specs=pl.BlockSpec((tm, tn), lambda i, j: (i, j)),
        out_shape=jax.ShapeDtypeStruct((m, n), F32),
        compiler_params=_params(("parallel", "arbitrary")),
        name="out_proj_residual",
    )(a, w, res)


def _layernorm(y, g, b):
    mu = jnp.mean(y, axis=-1, keepdims=True)
    yc = y - mu
    var = jnp.mean(yc * yc, axis=-1, keepdims=True)
    return yc * lax.rsqrt(var + LN_EPS) * g + b


def _ffn_up_kernel(y_ref, g_ref, b_ref, w_ref, cw_ref, cb_ref, o_ref, h_ref, u_ref, carry_ref, *, nj, tiles_per_seq):
    t = pl.program_id(0)
    last = pl.num_programs(0) - 1
    tm = y_ref.shape[0]
    half = o_ref.shape[1]
    pad = SUBLANES

    @pl.when(t == 0)
    def _():
        u_ref[...] = jnp.zeros(u_ref.shape, F32)
        carry_ref[...] = jnp.zeros(carry_ref.shape, F32)

    @pl.when((t % nj == 0) & (t < last))
    def _():
        h_ref[...] = _layernorm(y_ref[...], g_ref[...], b_ref[...]).astype(h_ref.dtype)

    tp = jnp.maximum(t - 1, 0)
    ip = tp // nj
    jp = tp % nj
    seq_start = (ip % tiles_per_seq == 0).astype(F32)
    u_ref[:pad, :] = carry_ref[jp] * (1.0 - seq_start)
    cw = cw_ref[...]
    c = cb_ref[...] + cw[2:3] * u_ref[pad:, :]
    for back in range(1, CONV_WIDTH):
        c = c + cw[CONV_WIDTH - 1 - back:CONV_WIDTH - back] * u_ref[pad - back:pad - back + tm, :]
    carry_ref[jp] = u_ref[tm:, :]
    gate = c[:, :half]
    o_ref[...] = (gate * jax.nn.sigmoid(gate) * c[:, half:]).astype(o_ref.dtype)

    u_ref[pad:, :] = _dot(h_ref[...], w_ref[...])


def _ffn_up(y, g, b, w_up, conv_w, conv_b, seq, tm, tn):
    m, d = y.shape
    n2 = w_up.shape[1]
    nj = n2 // (2 * tn)
    steps = (m // tm) * nj

    def cur(t):
        return jnp.minimum(t, steps - 1)

    def prev(t):
        return jnp.maximum(t - 1, 0)

    return pl.pallas_call(
        functools.partial(_ffn_up_kernel, nj=nj, tiles_per_seq=seq // tm),
        grid=(steps + 1,),
        in_specs=[pl.BlockSpec((tm, d), lambda t: (cur(t) // nj, 0)),
                  pl.BlockSpec((1, d), lambda t: (0, 0)),
                  pl.BlockSpec((1, d), lambda t: (0, 0)),
                  pl.BlockSpec((d, 2 * tn), lambda t: (0, cur(t) % nj)),
                  pl.BlockSpec((CONV_WIDTH, 2 * tn), lambda t: (0, prev(t) % nj)),
                  pl.BlockSpec((1, 2 * tn), lambda t: (0, prev(t) % nj))],
        out_specs=pl.BlockSpec((tm, tn), lambda t: (prev(t) // nj, prev(t) % nj)),
        out_shape=jax.ShapeDtypeStruct((m, n2 // 2), BF16),
        scratch_shapes=[pltpu.VMEM((tm, d), BF16),
                        pltpu.VMEM((tm + SUBLANES, 2 * tn), F32),
                        pltpu.VMEM((nj, SUBLANES, 2 * tn), F32)],
        compiler_params=_params(("arbitrary",)),
        name="ffn_up_conv_gate",
    )(y, g, b, w_up, conv_w, conv_b)


def _ffn_down_kernel(a_ref, w_ref, y_ref, g1_ref, b1_ref, g2_ref, b2_ref, o_ref, *, alpha):
    k = pl.program_id(1)
    tm, d = o_ref.shape

    @pl.when(k == 0)
    def _():
        o_ref[...] = jnp.zeros(o_ref.shape, F32)

    a = a_ref[...]
    for c in range(0, d, FFN_DOWN_COL_CHUNK):
        cols = slice(c, min(c + FFN_DOWN_COL_CHUNK, d))
        o_ref[:, cols] += _dot(a, w_ref[:, cols])

    @pl.when(k == pl.num_programs(1) - 1)
    def _():
        rows_per = min(tm, FFN_DOWN_ROW_CHUNK)

        def body(r, carry):
            rows = pl.ds(pl.multiple_of(r * rows_per, rows_per), rows_per)
            h = _layernorm(y_ref[rows, :], g1_ref[...], b1_ref[...])
            o_ref[rows, :] = _layernorm(alpha * h + o_ref[rows, :], g2_ref[...], b2_ref[...])
            return carry

        lax.fori_loop(0, tm // rows_per, body, 0)


def _ffn_down(act, w_down, y, g1, b1, g2, b2, alpha, tm, tk):
    m, kdim = act.shape
    d = w_down.shape[1]
    vec = pl.BlockSpec((1, d), lambda i, k: (0, 0))
    return pl.pallas_call(
        functools.partial(_ffn_down_kernel, alpha=alpha),
        grid=(m // tm, kdim // tk),
        in_specs=[pl.BlockSpec((tm, tk), lambda i, k: (i, k)),
                  pl.BlockSpec((tk, d), lambda i, k: (k, 0)),
                  pl.BlockSpec((tm, d), lambda i, k: (i, 0), pipeline_mode=pl.Buffered(1)),
                  vec, vec, vec, vec],
        out_specs=pl.BlockSpec((tm, d), lambda i, k: (i, 0)),
        out_shape=jax.ShapeDtypeStruct((m, d), F32),
        compiler_params=_params(("parallel", "arbitrary")),
        name="ffn_down_layernorm",
    )(act, w_down, y, g1, b1, g2, b2)


def _cast_kernel(x_ref, o_ref):
    o_ref[...] = x_ref[...].astype(o_ref.dtype)


def _cast_bf16(w, rb, cb, name):
    rows, cols = w.shape
    return pl.pallas_call(
        _cast_kernel,
        grid=(pl.cdiv(rows, rb), pl.cdiv(cols, cb)),
        in_specs=[pl.BlockSpec((rb, cb), lambda i, j: (i, j))],
        out_specs=pl.BlockSpec((rb, cb), lambda i, j: (i, j)),
        out_shape=jax.ShapeDtypeStruct((rows, cols), BF16),
        compiler_params=_params(("parallel", "parallel")),
        name=name,
    )(w)


def _cast_pad_kernel(x_ref, o_ref, *, n_src):
    keep = pl.program_id(0) < n_src
    o_ref[...] = jnp.where(keep, x_ref[...], 0.0).astype(o_ref.dtype)


def _cast_pad_rows_bf16(w, rows_out, rb, name):
    rows, cols = w.shape
    n_src = rows // rb
    return pl.pallas_call(
        functools.partial(_cast_pad_kernel, n_src=n_src),
        grid=(rows_out // rb,),
        in_specs=[pl.BlockSpec((rb, cols), lambda i: (jnp.minimum(i, n_src - 1), 0))],
        out_specs=pl.BlockSpec((rb, cols), lambda i: (i, 0)),
        out_shape=jax.ShapeDtypeStruct((rows_out, cols), BF16),
        compiler_params=_params(("parallel",)),
        name=name,
    )(w)


def _regroup_kernel(src_ref, keep_ref, *refs):
    x_refs, o_ref = refs[:-1], refs[-1]
    cb = x_refs[0].shape[1]
    for k, x_ref in enumerate(x_refs):
        keep = keep_ref[pl.program_id(0) * len(x_refs) + k] > 0
        o_ref[:, k * cb:(k + 1) * cb] = jnp.where(keep, x_ref[...], 0.0).astype(o_ref.dtype)


def _regroup_cast_w_up(w_up, f, tn, nj):
    d = w_up.shape[0]
    cb = math.gcd(f, tn)
    sub = tn // cb
    nsrc = f // cb
    src, keep = [], []
    for j in range(nj):
        for base in (0, nsrc):
            for r in range(sub):
                g = j * sub + r
                src.append(base + min(g, nsrc - 1))
                keep.append(1 if g < nsrc else 0)
    per_step = 2 * sub

    def src_spec(k):
        return pl.BlockSpec((d, cb), lambda b, src, keep: (0, src[b * per_step + k]))

    grid_spec = pltpu.PrefetchScalarGridSpec(
        num_scalar_prefetch=2,
        grid=(nj,),
        in_specs=[src_spec(k) for k in range(per_step)],
        out_specs=pl.BlockSpec((d, per_step * cb), lambda b, src, keep: (0, b)),
    )
    return pl.pallas_call(
        _regroup_kernel,
        grid_spec=grid_spec,
        out_shape=jax.ShapeDtypeStruct((d, nj * 2 * tn), BF16),
        compiler_params=_params(("parallel",)),
        name="w_up_regroup_cast",
    )(jnp.asarray(src, jnp.int32), jnp.asarray(keep, jnp.int32), *([w_up] * per_step))


def _rope_rows(wt):
    half = MLA_ROPE_DIM // 2
    return jnp.concatenate([-wt[half:], wt[:half]], axis=0)


def _pad_rows(w, n):
    return jnp.pad(w, ((0, n - w.shape[0]), (0, 0)))


def _pad_cols(w, n):
    return jnp.pad(w, ((0, 0), (0, n - w.shape[1])))


class _Tiles(NamedTuple):
    proj_m: int
    proj_n: int
    kv_up_m: int
    attn_q: int
    merge_m: int
    merge_n: int
    ffn_m: int
    ffn_n: int
    ffn_k: int
    cast: int


def _plan_tiles(m, d, seq, fp):
    return _Tiles(proj_m=_tile(m, 1024), proj_n=_tile(d, 1024), kv_up_m=_tile(m, 1024), attn_q=_tile(seq, 1024),
                  merge_m=_tile(m, 512), merge_n=_tile(d, 512), ffn_m=_tile(seq, 512), ffn_n=FFN_COLS_PER_STEP,
                  ffn_k=fp // 8 if fp % (8 * LANES) == 0 else _tile(fp, 1024), cast=1024)


def _layer(h, posb, pos_rows, batch, seq, alpha, lambda_init, w_in, b_gate, lam_p, da_g, q_g, kv_g, w_uq, w_ukv,
           w_proj_a, w_proj_b, w_out, ln1_g, ln1_b, w_up, conv_w, conv_b, w_down, ln2_g, ln2_b):
    m, d = h.shape
    f = w_down.shape[0]
    fp = -(-f // FFN_COLS_PER_STEP) * FFN_COLS_PER_STEP
    t = _plan_tiles(m, d, seq, fp)
    da_cols = 3 * DA_HEADS * DA_V_DIM
    off_cq = da_cols
    off_ckv = off_cq + MLA_Q_RANK
    off_kr = off_ckv + MLA_KV_RANK
    off_gate = off_kr + MLA_ROPE_DIM

    hb = h.astype(BF16)

    da_scale = DA_HEAD_DIM ** -0.5 * LOG2E
    qkv_scale = jnp.concatenate([jnp.full((DA_HEADS * DA_V_DIM,), da_scale, F32),
                                 jnp.ones((2 * DA_HEADS * DA_V_DIM,), F32)])[None, :]
    wt_in = _cast_bf16(jnp.swapaxes(w_in, 0, 1), t.cast, d, "w_in_cast")
    qkv = _mm_scale(hb, wt_in, qkv_scale, BF16, t.proj_m, t.proj_n, "da_qkv_proj")

    wt_kr = wt_in[off_kr:off_gate]
    wt_lat = jnp.concatenate([wt_in[off_cq:off_kr], _pad_rows(wt_kr, LANES), _pad_rows(_rope_rows(wt_kr), LANES)], axis=0)
    lat = _mm_scale(hb, wt_lat, jnp.ones((1, wt_lat.shape[0]), F32), F32, t.proj_m, wt_lat.shape[0] // 2,
                    "mla_latent_proj")

    slopes = 2.0 ** (-8.0 * jnp.arange(1, DA_HEADS + 1, dtype=F32) / DA_HEADS)
    oa = _diff_attention(qkv, pos_rows, lam_p, da_g[None, :], slopes, batch, seq, t.attn_q, lambda_init)

    half = MLA_ROPE_DIM // 2
    inv = ROPE_THETA ** (-jnp.arange(half, dtype=F32) / half)
    inv_row = _pad_cols(jnp.concatenate([inv, inv])[None, :], LANES)
    hd = MLA_NOPE_DIM + MLA_ROPE_DIM
    wq = w_uq.reshape(MLA_Q_RANK, MLA_HEADS, hd)
    zeros = jnp.zeros((MLA_Q_RANK, MLA_HEADS, LANES - MLA_ROPE_DIM), F32)
    wq_main = jnp.concatenate([wq, zeros], axis=2).reshape(MLA_Q_RANK, MLA_HEADS * 2 * LANES).astype(BF16)
    wq_swap = jnp.concatenate([-wq[:, :, MLA_NOPE_DIM + half:], wq[:, :, MLA_NOPE_DIM:MLA_NOPE_DIM + half], zeros],
                              axis=2).reshape(MLA_Q_RANK, MLA_HEADS * LANES).astype(BF16)
    qcat = _q_up(lat, q_g[None, :], posb, inv_row, wq_main, wq_swap, hd ** -0.5 * LOG2E, t.proj_m)
    kcat, vext = _kv_up(lat, kv_g[None, :], posb, inv_row, w_ukv.astype(BF16), t.kv_up_m)
    ob = _mla_attention(qcat, kcat, vext, batch, seq, t.attn_q)

    merged = _gated_merge(hb, oa, ob, wt_in[off_gate:off_gate + d], wt_in[off_gate + d:],
                          w_proj_a.astype(BF16), w_proj_b.astype(BF16), b_gate[None, :d], b_gate[None, d:],
                          t.merge_m, t.merge_n)
    y1 = _mm_res(merged, w_out.astype(BF16), h, alpha, t.proj_m, t.proj_n)

    tn = t.ffn_n
    nj = fp // tn

    def group(a):
        r = a.shape[0]
        gv = jnp.stack([_pad_cols(a[:, :f], fp).reshape(r, nj, tn), _pad_cols(a[:, f:], fp).reshape(r, nj, tn)], axis=2)
        return gv.reshape(r, nj * 2 * tn)

    act = _ffn_up(y1, ln1_g[None, :], ln1_b[None, :], _regroup_cast_w_up(w_up, f, tn, nj), group(conv_w),
                  group(conv_b[None, :]), seq, t.ffn_m, tn)
    w_down_p = _cast_pad_rows_bf16(w_down, fp, math.gcd(f, tn), "w_down_cast")
    return _ffn_down(act, w_down_p, y1, ln1_g[None, :], ln1_b[None, :], ln2_g[None, :], ln2_b[None, :], alpha,
                     t.ffn_m, t.ffn_k)


def kernel(x, positions, w_in, b_gate, da_lambda_q1, da_lambda_k1, da_lambda_q2, da_lambda_k2, da_subln_g, mla_q_norm_g, mla_kv_norm_g, w_uq, w_ukv, w_proj_a, w_proj_b, w_out, ln1_g, ln1_b, w_up, conv_w, conv_b, w_down, ln2_g, ln2_b):
    batch, seq, d = x.shape
    depth = w_in.shape[0]
    alpha = (2.0 * depth) ** 0.25
    m = batch * seq
    posf = positions.astype(F32).reshape(m)
    posb = jnp.broadcast_to(posf[:, None], (m, LANES))
    pos_rows = posf.reshape(batch, 1, seq)
    h = x.reshape(m, d)
    for l in range(depth):
        lambda_init = 0.8 - 0.6 * math.exp(-0.3 * l)
        lam_p = jnp.stack([da_lambda_q1[l], da_lambda_k1[l], da_lambda_q2[l], da_lambda_k2[l]]).astype(F32)
        h = _layer(h, posb, pos_rows, batch, seq, alpha, lambda_init, w_in[l], b_gate[l], lam_p, da_subln_g[l],
                   mla_q_norm_g[l], mla_kv_norm_g[l], w_uq[l], w_ukv[l], w_proj_a[l], w_proj_b[l], w_out[l],
                   ln1_g[l], ln1_b[l], w_up[l], conv_w[l], conv_b[l], w_down[l], ln2_g[l], ln2_b[l])
    return h.reshape(batch, seq, d)
```
